```python
import jax, jax.numpy as jnp
from jax import lax
import numpy as np


D_MODEL = 2048
BATCH = 4
SEQ = 4096
DEPTH = 2

MEM_TOKENS = 256
CONV_W = 4
CONV_PAD = ((CONV_W - 1) // 2, CONV_W // 2)

SSD_D = D_MODEL
SSD_HEAD_DIM = 64
SSD_HEADS = SSD_D // SSD_HEAD_DIM
SSD_GROUPS = 4
SSD_HPG = SSD_HEADS // SSD_GROUPS
SSD_STATE = 128
SSD_CHUNK = 128
SSD_CONV_CH = SSD_D + 2 * SSD_GROUPS * SSD_STATE

GLA_HEADS = 4
GLA_DK = D_MODEL // 2 // GLA_HEADS
GLA_DV = D_MODEL // GLA_HEADS
GLA_RANK = 16
GLA_NORMALIZER = 16.0
GLA_CHUNK = 64

EVEN_SIZES = (SSD_D, SSD_CONV_CH, 2 * SSD_HEADS, GLA_HEADS * GLA_DK, GLA_HEADS * GLA_DK, GLA_HEADS * GLA_DV, GLA_HEADS * GLA_DV, 2 * GLA_RANK)
EVEN_IN = sum(EVEN_SIZES)
EVEN_MIX = SSD_D + GLA_HEADS * GLA_DV

RNN_WIDTH = 5 * D_MODEL // 4
RNN_BLOCKS = 10
RNN_BLOCK = RNN_WIDTH // RNN_BLOCKS
RG_C = 8.0

XATTN_HEADS = 4
XATTN_HEAD_DIM = D_MODEL // XATTN_HEADS

N_EXPERTS = 32
TOP_K = 4
EXPERT_FF = D_MODEL
SWIGLU_ALPHA = 1.702
SWIGLU_LIMIT = 7.0
MOE_BLOCK = 512

DEEPNORM_ALPHA = (2 * DEPTH) ** 0.25
DEEPNORM_BETA = (8 * DEPTH) ** -0.25
N_EVEN = (DEPTH + 1) // 2
N_ODD = DEPTH // 2
LN_EPS = 1e-5
RMS_EPS = 1e-6

kernel_name = 'hybrid_ssd_gla_rglru_moe_encoder'


def layer_norm(x, g, b):
    xf = x.astype(jnp.float32)
    mu = jnp.mean(xf, axis=-1, keepdims=True)
    var = jnp.mean(jnp.square(xf - mu), axis=-1, keepdims=True)
    return ((xf - mu) * lax.rsqrt(var + LN_EPS) * g + b).astype(x.dtype)


def rms_norm(x, g):
    xf = x.astype(jnp.float32)
    return (xf * lax.rsqrt(jnp.mean(jnp.square(xf), axis=-1, keepdims=True) + RMS_EPS) * g).astype(x.dtype)


def dw_conv(x, w, b):
    c = x.shape[-1]
    y = lax.conv_general_dilated(x, w[:, None, :], window_strides=(1,), padding=[CONV_PAD],
                                 dimension_numbers=('NWC', 'WIO', 'NWC'), feature_group_count=c)
    return y + b


def flip(t):
    return jnp.flip(t, axis=1)


def to_chunks(t, q):
    b, l = t.shape[:2]
    return jnp.moveaxis(t.reshape((b, l // q, q) + t.shape[2:]), 1, 0)


def from_chunks(t):
    c, b, q = t.shape[:3]
    return jnp.moveaxis(t, 0, 1).reshape((b, c * q) + t.shape[3:])


def ssd_scan(x, dt, A, Bm, Cm):
    xc, dtc, bc, cc = (to_chunks(t, SSD_CHUNK) for t in (x, dt, Bm, Cm))
    a_cum = jnp.cumsum(dtc * A, axis=2)
    mask = jnp.tril(jnp.ones((SSD_CHUNK, SSD_CHUNK), dtype=bool))
    seg = a_cum[:, :, :, None] - a_cum[:, :, None]
    decay = jnp.exp(jnp.where(mask[:, :, None, None], seg, -jnp.inf))
    cb = jnp.einsum('cbqgn,cbkgn->cbqkg', cc, bc)
    w = cb[..., None] * decay * dtc[:, :, None]
    y_diag = jnp.einsum('cbqkgr,cbkgrp->cbqgrp', w, xc)
    xw = xc * (jnp.exp(a_cum[:, :, -1:] - a_cum) * dtc)[..., None]
    a_last = a_cum[:, :, -1]

    def step(h, inp):
        cq, bk, xwk, acq, al = inp
        y_off = jnp.einsum('bqgn,bgrpn->bqgrp', cq, h) * jnp.exp(acq)[..., None]
        h = h * jnp.exp(al)[..., None, None] + jnp.einsum('bkgn,bkgrp->bgrpn', bk, xwk)
        return h, y_off

    h0 = jnp.zeros((x.shape[0],) + x.shape[2:] + (Bm.shape[-1],), x.dtype)
    _, y_off = lax.scan(step, h0, (cc, bc, xw, a_cum, a_last))
    return from_chunks(y_diag + y_off)


def gla_scan(q, k, v, g):
    qc, kc, vc, gc = (to_chunks(t, GLA_CHUNK) for t in (q, k, v, g))
    bcum = jnp.cumsum(gc, axis=2)
    q_in = qc * jnp.exp(bcum)
    k_in = kc * jnp.exp(-bcum)
    mask = jnp.tril(jnp.ones((GLA_CHUNK, GLA_CHUNK), dtype=bool))
    att = jnp.where(mask, jnp.einsum('cbqhd,cbkhd->cbhqk', q_in, k_in), 0.0)
    o_intra = jnp.einsum('cbhqk,cbkhv->cbqhv', att, vc)
    b_last = bcum[:, :, -1]
    k_st = kc * jnp.exp(b_last[:, :, None] - bcum)

    def step(s, inp):
        qi, ks, vs, bl = inp
        o = jnp.einsum('bqhd,bhdv->bqhv', qi, s)
        s = s * jnp.exp(bl)[..., None] + jnp.einsum('bkhd,bkhv->bhdv', ks, vs)
        return s, o

    s0 = jnp.zeros((q.shape[0], q.shape[2], q.shape[3], v.shape[3]), q.dtype)
    _, o_inter = lax.scan(step, s0, (q_in, k_st, vc, b_last))
    return from_chunks(o_intra + o_inter)


def ssd_gla_mixer(x, w_in, conv_w, conv_b, a_log, dt_bias, d_skip, ssd_norm, gk_w2, gk_b, gla_norm, w_out):
    f32 = jnp.float32
    b, l, _ = x.shape
    offsets = [int(o) for o in np.cumsum(EVEN_SIZES)[:-1]]
    z, xbc, dt_raw, q, k, v, g_out, gk_lr = jnp.split(x @ w_in, offsets, axis=-1)

    xbc = jax.nn.silu(dw_conv(xbc, conv_w, conv_b))
    xs, bm, cm = jnp.split(xbc, [SSD_D, SSD_D + SSD_GROUPS * SSD_STATE], axis=-1)
    xs = xs.reshape(b, l, SSD_GROUPS, SSD_HPG, SSD_HEAD_DIM).astype(f32)
    bm = bm.reshape(b, l, SSD_GROUPS, SSD_STATE).astype(f32)
    cm = cm.reshape(b, l, SSD_GROUPS, SSD_STATE).astype(f32)
    dt = jax.nn.softplus((dt_raw.reshape(b, l, 2, SSD_HEADS) + dt_bias).astype(f32))
    dt = dt.reshape(b, l, 2, SSD_GROUPS, SSD_HPG)
    A = -jnp.exp(a_log.astype(f32)).reshape(2, SSD_GROUPS, SSD_HPG)
    y = ssd_scan(xs, dt[:, :, 0], A[0], bm, cm) + flip(ssd_scan(flip(xs), flip(dt[:, :, 1]), A[1], flip(bm), flip(cm)))
    y = y + d_skip.reshape(SSD_GROUPS, SSD_HPG, 1) * xs
    y = y.reshape(b, l, SSD_GROUPS, SSD_HPG * SSD_HEAD_DIM) * jax.nn.silu(z.astype(f32).reshape(b, l, SSD_GROUPS, -1))
    y = rms_norm(y, ssd_norm.reshape(SSD_GROUPS, -1)).reshape(b, l, SSD_D)

    q = q.reshape(b, l, GLA_HEADS, GLA_DK).astype(f32) * GLA_DK ** -0.5
    k = k.reshape(b, l, GLA_HEADS, GLA_DK).astype(f32)
    v = v.reshape(b, l, GLA_HEADS, GLA_DV).astype(f32)
    gk = jnp.einsum('bldr,drk->bldk', gk_lr.reshape(b, l, 2, GLA_RANK), gk_w2) + gk_b
    gk = (jax.nn.log_sigmoid(gk.astype(f32)) / GLA_NORMALIZER).reshape(b, l, 2, GLA_HEADS, GLA_DK)
    o = gla_scan(q, k, v, gk[:, :, 0]) + flip(gla_scan(flip(q), flip(k), flip(v), flip(gk[:, :, 1])))
    o = rms_norm(o, gla_norm).reshape(b, l, GLA_HEADS * GLA_DV) * jax.nn.silu(g_out.astype(f32))

    mixed = jnp.concatenate([y, o], axis=-1).astype(x.dtype)
    return mixed @ w_out


def rglru_scan(v, wa, ba, wx, bx, lam, reverse):
    b, l, _ = v.shape
    vb = v.reshape(b, l, RNN_BLOCKS, RNN_BLOCK)
    r = jax.nn.sigmoid(jnp.einsum('blnc,ncd->blnd', vb, wa).reshape(b, l, RNN_WIDTH) + ba)
    i = jax.nn.sigmoid(jnp.einsum('blnc,ncd->blnd', vb, wx).reshape(b, l, RNN_WIDTH) + bx)
    log_a = -RG_C * r * jax.nn.softplus(-lam)
    a = jnp.exp(log_a)
    u = jnp.sqrt(-jnp.expm1(2.0 * log_a)) * (i * v)

    def combine(left, right):
        a1, b1 = left
        a2, b2 = right
        return a1 * a2, a2 * b1 + b2

    _, h = lax.associative_scan(combine, (a, u), axis=1, reverse=reverse)
    return h


def rglru_mixer(x, w_in, conv_w, conv_b, wa, ba, wx, bx, lam, w_out):
    f32 = jnp.float32
    gate, v = jnp.split(x @ w_in, 2, axis=-1)
    v = dw_conv(v, conv_w, conv_b).astype(f32)
    h = rglru_scan(v, wa[0], ba[0], wx[0], bx[0], lam[0], False) + rglru_scan(v, wa[1], ba[1], wx[1], bx[1], lam[1], True)
    y = (jax.nn.gelu(gate.astype(f32)) * h).astype(x.dtype)
    return y @ w_out


def memory_cross_attention(x, mem, wq, wkv, wo):
    b, l, d = x.shape
    m = mem.shape[1]
    q = (x @ wq).reshape(b, l, XATTN_HEADS, XATTN_HEAD_DIM)
    k, v = jnp.split(mem @ wkv, 2, axis=-1)
    k = k.reshape(b, m, XATTN_HEADS, XATTN_HEAD_DIM)
    v = v.reshape(b, m, XATTN_HEADS, XATTN_HEAD_DIM)
    s = jnp.einsum('blhd,bmhd->bhlm', q, k).astype(jnp.float32) * XATTN_HEAD_DIM ** -0.5
    p = jax.nn.softmax(s, axis=-1).astype(x.dtype)
    o = jnp.einsum('bhlm,bmhd->blhd', p, v).reshape(b, l, d)
    return o @ wo


def moe_ffn(x, router_w, router_b, w_gu, b_gu, w_dn, b_dn):
    b, l, d = x.shape
    xt = x.reshape(b * l, d)
    t = b * l
    n_assign = t * TOP_K
    logits = (xt @ router_w + router_b).astype(jnp.float32)
    top_v, top_i = lax.top_k(logits, TOP_K)
    gates = jax.nn.softmax(top_v, axis=-1).astype(x.dtype)
    flat_e = top_i.reshape(-1)
    flat_tok = jnp.arange(n_assign, dtype=jnp.int32) // TOP_K
    flat_w = gates.reshape(-1)
    order = jnp.argsort(flat_e)
    se = flat_e[order]
    counts = jnp.bincount(flat_e, length=N_EXPERTS)
    starts = jnp.cumsum(counts) - counts
    pcounts = (counts + MOE_BLOCK - 1) // MOE_BLOCK * MOE_BLOCK
    pends = jnp.cumsum(pcounts)
    pstarts = pends - pcounts
    dest = pstarts[se] + (jnp.arange(n_assign, dtype=jnp.int32) - starts[se])
    n_blocks = -(-n_assign // MOE_BLOCK) + N_EXPERTS
    n_rows = n_blocks * MOE_BLOCK
    row_tok = jnp.zeros((n_rows,), jnp.int32).at[dest].set(flat_tok[order])
    row_w = jnp.zeros((n_rows,), x.dtype).at[dest].set(flat_w[order])
    block_e = jnp.minimum(jnp.searchsorted(pends, jnp.arange(n_blocks, dtype=jnp.int32) * MOE_BLOCK, side='right'), N_EXPERTS - 1)

    def expert_block(args):
        tok, e = args
        h = xt[tok] @ w_gu[e] + b_gu[e]
        glu, lin = jnp.split(h, 2, axis=-1)
        glu = jnp.minimum(glu, SWIGLU_LIMIT)
        lin = jnp.clip(lin, -SWIGLU_LIMIT, SWIGLU_LIMIT)
        act = (lin + 1.0) * glu * jax.nn.sigmoid(SWIGLU_ALPHA * glu)
        return act @ w_dn[e] + b_dn[e]

    out = lax.map(expert_block, (row_tok.reshape(n_blocks, MOE_BLOCK), block_e)).reshape(n_rows, d)
    y = jax.ops.segment_sum(out * row_w[:, None], row_tok, num_segments=t)
    return y.reshape(b, l, d)


def setup_inputs(seed: int = 0) -> dict:
    key = jax.random.key(seed)
    ks = jax.random.split(key, 33)
    f32 = jnp.float32

    def nrm(k, shape, scale):
        return jax.random.normal(k, shape, f32) * scale

    dt0 = jnp.exp(jax.random.uniform(ks[8], (N_EVEN, 2, SSD_HEADS), f32, minval=float(np.log(1e-3)), maxval=float(np.log(1e-1))))
    a0 = jax.random.uniform(ks[22], (N_ODD, 2, RNN_WIDTH), f32, minval=0.9, maxval=0.999)
    return {
        'x': nrm(ks[0], (BATCH, SEQ, D_MODEL), 1.0),
        'mem': nrm(ks[1], (BATCH, MEM_TOKENS, D_MODEL), 1.0),
        'ln_g': 1.0 + nrm(ks[2], (DEPTH, 3, D_MODEL), 0.02),
        'ln_b': nrm(ks[3], (DEPTH, 3, D_MODEL), 0.02),
        'even_w_in': nrm(ks[4], (N_EVEN, D_MODEL, EVEN_IN), D_MODEL ** -0.5),
        'even_conv_w': nrm(ks[5], (N_EVEN, CONV_W, SSD_CONV_CH), CONV_W ** -0.5),
        'even_conv_b': nrm(ks[6], (N_EVEN, SSD_CONV_CH), 0.02),
        'ssd_a_log': jnp.log(jax.random.uniform(ks[7], (N_EVEN, 2, SSD_HEADS), f32, minval=1.0, maxval=16.0)),
        'ssd_dt_bias': dt0 + jnp.log(-jnp.expm1(-dt0)),
        'ssd_d': 1.0 + nrm(ks[9], (N_EVEN, SSD_HEADS), 0.02),
        'ssd_norm': 1.0 + nrm(ks[10], (N_EVEN, SSD_D), 0.02),
        'gla_gk_w2': nrm(ks[11], (N_EVEN, 2, GLA_RANK, GLA_HEADS * GLA_DK), GLA_RANK ** -0.5),
        'gla_gk_b': nrm(ks[12], (N_EVEN, 2, GLA_HEADS * GLA_DK), 0.1),
        'gla_norm': 1.0 + nrm(ks[13], (N_EVEN, GLA_DV), 0.02),
        'even_w_out': nrm(ks[14], (N_EVEN, EVEN_MIX, D_MODEL), EVEN_MIX ** -0.5 * DEEPNORM_BETA),
        'odd_w_in': nrm(ks[15], (N_ODD, D_MODEL, 2 * RNN_WIDTH), D_MODEL ** -0.5),
        'odd_conv_w': nrm(ks[16], (N_ODD, CONV_W, RNN_WIDTH), CONV_W ** -0.5),
        'odd_conv_b': nrm(ks[17], (N_ODD, RNN_WIDTH), 0.02),
        'rg_wa': nrm(ks[18], (N_ODD, 2, RNN_BLOCKS, RNN_BLOCK, RNN_BLOCK), RNN_BLOCK ** -0.5),
        'rg_ba': nrm(ks[19], (N_ODD, 2, RNN_WIDTH), 0.02),
        'rg_wx': nrm(ks[20], (N_ODD, 2, RNN_BLOCKS, RNN_BLOCK, RNN_BLOCK), RNN_BLOCK ** -0.5),
        'rg_bx': nrm(ks[21], (N_ODD, 2, RNN_WIDTH), 0.02),
        'rg_lam': jnp.log(a0) - jnp.log1p(-a0),
        'odd_w_out': nrm(ks[23], (N_ODD, RNN_WIDTH, D_MODEL), RNN_WIDTH ** -0.5 * DEEPNORM_BETA),
        'xattn_wq': nrm(ks[24], (DEPTH, D_MODEL, D_MODEL), D_MODEL ** -0.5),
        'xattn_wkv': nrm(ks[25], (DEPTH, D_MODEL, 2 * D_MODEL), D_MODEL ** -0.5),
        'xattn_wo': nrm(ks[26], (DEPTH, D_MODEL, D_MODEL), D_MODEL ** -0.5 * DEEPNORM_BETA),
        'router_w': nrm(ks[27], (DEPTH, D_MODEL, N_EXPERTS), D_MODEL ** -0.5),
        'router_b': nrm(ks[28], (DEPTH, N_EXPERTS), 0.01),
        'moe_w_gu': nrm(ks[29], (DEPTH, N_EXPERTS, D_MODEL, 2 * EXPERT_FF), D_MODEL ** -0.5),
        'moe_b_gu': nrm(ks[30], (DEPTH, N_EXPERTS, 2 * EXPERT_FF), 0.02),
        'moe_w_dn': nrm(ks[31], (DEPTH, N_EXPERTS, EXPERT_FF, D_MODEL), EXPERT_FF ** -0.5 * DEEPNORM_BETA),
        'moe_b_dn': nrm(ks[32], (DEPTH, N_EXPERTS, D_MODEL), 0.02),
    }


def reference(x, mem, ln_g, ln_b, even_w_in, even_conv_w, even_conv_b, ssd_a_log, ssd_dt_bias, ssd_d, ssd_norm,
              gla_gk_w2, gla_gk_b, gla_norm, even_w_out, odd_w_in, odd_conv_w, odd_conv_b, rg_wa, rg_ba, rg_wx,
              rg_bx, rg_lam, odd_w_out, xattn_wq, xattn_wkv, xattn_wo, router_w, router_b, moe_w_gu, moe_b_gu,
              moe_w_dn, moe_b_dn):
    for layer in range(DEPTH):
        i = layer // 2
        if layer % 2 == 0:
            h = ssd_gla_mixer(x, even_w_in[i], even_conv_w[i], even_conv_b[i], ssd_a_log[i], ssd_dt_bias[i], ssd_d[i],
                              ssd_norm[i], gla_gk_w2[i], gla_gk_b[i], gla_norm[i], even_w_out[i])
        else:
            h = rglru_mixer(x, odd_w_in[i], odd_conv_w[i], odd_conv_b[i], rg_wa[i], rg_ba[i], rg_wx[i], rg_bx[i],
                            rg_lam[i], odd_w_out[i])
        x = layer_norm(DEEPNORM_ALPHA * x + h, ln_g[layer, 0], ln_b[layer, 0])
        h = memory_cross_attention(x, mem, xattn_wq[layer], xattn_wkv[layer], xattn_wo[layer])
        x = layer_norm(DEEPNORM_ALPHA * x + h, ln_g[layer, 1], ln_b[layer, 1])
        h = moe_ffn(x, router_w[layer], router_b[layer], moe_w_gu[layer], moe_b_gu[layer], moe_w_dn[layer], moe_b_dn[layer])
        x = layer_norm(DEEPNORM_ALPHA * x + h, ln_g[layer, 2], ln_b[layer, 2])
    return x
```

```python
import functools

import jax
import jax.numpy as jnp
from jax import lax
from jax.experimental import pallas as pl
from jax.experimental.pallas import tpu as pltpu

F32 = jnp.float32
BF16 = jnp.bfloat16
I32 = jnp.int32

SSD_HEAD_DIM = 64
SSD_GROUPS = 4
SSD_CHUNK = 128
GLA_HEADS = 4
GLA_NORMALIZER = 16.0
GLA_CHUNK = 64
RG_C = 8.0
XATTN_HEADS = 4
TOP_K = 4
SWIGLU_ALPHA = 1.702
SWIGLU_LIMIT = 7.0
LN_EPS = 1e-5
RMS_EPS = 1e-6

LANES = 128
VMEM_LIMIT_BYTES = 56 * 1024 * 1024
CONV_HALO = 16
MOE_TILE = 1024
MOE_FF_TILE = 256
MOE_SUB = 256
NEG_BIG = -3.0e38
HI = lax.Precision.HIGHEST

TN_DIMS = (((0,), (0,)), ((), ()))
NT_DIMS = (((1,), (1,)), ((), ()))


def _params(*sem):
    return pltpu.CompilerParams(dimension_semantics=sem, vmem_limit_bytes=VMEM_LIMIT_BYTES)


def _dot(a, b):
    return jnp.dot(a, b, preferred_element_type=F32)


def _dot_nt(a, b):
    return lax.dot_general(a, b, NT_DIMS, preferred_element_type=F32)


def _dot_tn(a, b):
    return lax.dot_general(a, b, TN_DIMS, preferred_element_type=F32)


def _sigmoid(x):
    return 1.0 / (1.0 + jnp.exp(-x))


def _softplus(x):
    return jnp.maximum(x, 0.0) + jnp.log(1.0 + jnp.exp(-jnp.abs(x)))


def _layer_norm(v, g, b):
    mu = jnp.mean(v, axis=-1, keepdims=True)
    c = v - mu
    var = jnp.mean(c * c, axis=-1, keepdims=True)
    return c * lax.rsqrt(var + LN_EPS) * g + b


def _pack_rows(v):
    h = v.shape[1] // 2
    lo = pltpu.bitcast(v[:, :h].astype(BF16).astype(F32), I32)
    hi = pltpu.bitcast(v[:, h:].astype(BF16).astype(F32), I32)
    return (lo & jnp.int32(-65536)) | (lax.shift_right_logical(hi, jnp.int32(16)))


def _unpack_rows(w):
    lo = pltpu.bitcast(w & jnp.int32(-65536), F32)
    hi = pltpu.bitcast(lax.shift_left(w, jnp.int32(16)), F32)
    return lo, hi


def _largest_tile(n, cap, *also):
    for c in range(min(cap, n) // LANES * LANES, 0, -LANES):
        if n % c == 0 and all(a % c == 0 for a in also):
            return c
    return n


def _mm_body(a_ref, w_ref, o_ref):
    o_ref[...] = _dot(a_ref[...], w_ref[...]).astype(o_ref.dtype)


def _matmul(a, w, out_dtype, tm=1024, tn=1024):
    m, k = a.shape
    n = w.shape[1]
    tm = min(tm, m)
    tn = _largest_tile(n, tn)
    assert m % tm == 0
    return pl.pallas_call(
        _mm_body,
        grid=(n // tn, m // tm),
        in_specs=[pl.BlockSpec((tm, k), lambda j, i: (i, 0)), pl.BlockSpec((k, tn), lambda j, i: (0, j))],
        out_specs=pl.BlockSpec((tm, tn), lambda j, i: (i, j)),
        out_shape=jax.ShapeDtypeStruct((m, n), out_dtype),
        compiler_params=_params("parallel", "parallel"),
        name="proj_matmul",
    )(a, w)


def _conv_body(x_ref, w_ref, b_ref, o_ref, *, act, rows):
    seq = x_ref.shape[0]
    nchunks = seq // rows
    tot = rows + 2 * CONV_HALO
    w = w_ref[...]
    b = b_ref[...]

    def body(i, carry):
        r0 = pl.multiple_of(i * rows, rows)
        main = x_ref[pl.ds(r0, rows), :].astype(F32)
        p0 = pl.multiple_of(jnp.maximum(r0 - CONV_HALO, 0), CONV_HALO)
        prev = x_ref[pl.ds(p0, CONV_HALO), :].astype(F32)
        prev = jnp.where(i > 0, prev, 0.0)
        n0 = pl.multiple_of(jnp.minimum(r0 + rows, seq - CONV_HALO), CONV_HALO)
        nxt = x_ref[pl.ds(n0, CONV_HALO), :].astype(F32)
        nxt = jnp.where(i < nchunks - 1, nxt, 0.0)
        ext = jnp.concatenate([prev, main, nxt], axis=0)
        acc = b + w[1:2] * main
        for j in (0, 2, 3):
            shift = (1 - j) % tot
            acc = acc + w[j:j + 1] * pltpu.roll(ext, shift, 0)[CONV_HALO:CONV_HALO + rows]
        if act:
            acc = acc * _sigmoid(acc)
        o_ref[pl.ds(r0, rows), :] = acc.astype(o_ref.dtype)
        return carry

    lax.fori_loop(0, nchunks, body, 0)


def _dwconv(x, col0, width, conv_w, conv_b, batch, act, cw=512):
    t = x.shape[0]
    seq = t // batch
    cw = _largest_tile(width, cw, col0)
    assert width % cw == 0 and col0 % cw == 0
    rows = min(256, seq)
    cb0 = col0 // cw
    return pl.pallas_call(
        functools.partial(_conv_body, act=act, rows=rows),
        grid=(batch, width // cw),
        in_specs=[
            pl.BlockSpec((seq, cw), lambda b, j: (b, cb0 + j)),
            pl.BlockSpec((conv_w.shape[0], cw), lambda b, j: (0, j)),
            pl.BlockSpec((1, cw), lambda b, j: (0, j)),
        ],
        out_specs=pl.BlockSpec((seq, cw), lambda b, j: (b, j)),
        out_shape=jax.ShapeDtypeStruct((t, width), BF16),
        compiler_params=_params("parallel", "parallel"),
        name="dwconv",
    )(x, conv_w, conv_b.reshape(1, -1))


def _ssd_body(xs_ref, b_ref, c_ref, z_ref, ps_ref, alog_ref, dtb_ref, dsk_ref, nrm_ref, y_ref, yacc_ref, h_ref):
    seq, hp = xs_ref.shape
    q = SSD_CHUNK
    nc = seq // q
    p = SSD_HEAD_DIM
    r_heads = hp // p
    a_neg = -jnp.exp(alog_ref[0])
    dtb = dtb_ref[0]
    row = lax.broadcasted_iota(I32, (q, q), 0)
    col = lax.broadcasted_iota(I32, (q, q), 1)
    lower = (col <= row).astype(F32)
    upper = (col >= row).astype(F32)
    e_row = lax.broadcasted_iota(I32, (LANES, hp), 0)
    e_col = lax.broadcasted_iota(I32, (LANES, hp), 1) // p
    pw = min(LANES, hp)
    piece_head = lax.broadcasted_iota(I32, (q, pw), 1) // p

    for dirn in (0, 1):
        h_ref[...] = jnp.zeros_like(h_ref)
        tri, tri_t = (lower, upper) if dirn == 0 else (upper, lower)
        mask = (col <= row) if dirn == 0 else (col >= row)
        expand = (e_row == e_col + dirn * r_heads).astype(BF16)
        edge = q - 1 if dirn == 0 else 0

        def chunk(ci, carry, dirn=dirn, tri=tri, tri_t=tri_t, mask=mask, expand=expand, edge=edge):
            c = ci if dirn == 0 else nc - 1 - ci
            r0 = pl.multiple_of(c * q, q)
            x = xs_ref[pl.ds(r0, q), :]
            xf = x.astype(F32)
            bm = b_ref[pl.ds(r0, q), :]
            cm = c_ref[pl.ds(r0, q), :]
            dt = _softplus(ps_ref[pl.ds(r0, q), :] + dtb)
            dta = dt * a_neg
            acum = jnp.dot(tri, dta, precision=HI, preferred_element_type=F32)
            acum_t = lax.dot_general(dta, tri_t, TN_DIMS, precision=HI, preferred_element_type=F32)
            alast = acum[edge:edge + 1, :]
            stack = jnp.concatenate([dt, jnp.exp(acum), dt * jnp.exp(alast - acum)], axis=0).astype(BF16)
            ex = _dot(stack, expand)
            xdt = (xf * ex[0:q]).astype(BF16)
            eac = ex[q:2 * q]
            xw = (xf * ex[2 * q:3 * q]).astype(BF16)
            cb = _dot_nt(cm, bm)
            yd = []
            for j in range(hp // pw):
                xpiece = xdt[:, j * pw:(j + 1) * pw]
                ypiece = None
                for a in range(pw // p):
                    cr = dirn * r_heads + j * (pw // p) + a
                    seg = acum[:, cr:cr + 1] - acum_t[cr:cr + 1, :]
                    dec = jnp.where(mask, jnp.exp(jnp.minimum(seg, 0.0)), 0.0)
                    wgt = (cb * dec).astype(BF16)
                    xa = xpiece if pw == p else jnp.where(piece_head == a, xpiece, jnp.zeros_like(xpiece))
                    ya = _dot(wgt, xa)
                    ypiece = ya if ypiece is None else ypiece + ya
                yd.append(ypiece)
            y = jnp.concatenate(yd, axis=1) if len(yd) > 1 else yd[0]
            hst = h_ref[...]
            y = y + _dot(cm, hst.astype(BF16)) * eac
            ea = jnp.broadcast_to(jnp.exp(alast), (8, LANES))
            ea_hi = ea.astype(BF16)
            ea_lo = (ea - ea_hi.astype(F32)).astype(BF16)
            hdec = (_dot(ea_hi, expand) + _dot(ea_lo, expand))[0:1]
            h_ref[...] = hst * hdec + _dot_tn(bm, xw)
            if dirn == 0:
                yacc_ref[pl.ds(r0, q), :] = y
            else:
                yt = yacc_ref[pl.ds(r0, q), :] + y + dsk_ref[...] * xf
                zf = z_ref[pl.ds(r0, q), :].astype(F32)
                yt = yt * (zf * _sigmoid(zf))
                ms = jnp.mean(yt * yt, axis=-1, keepdims=True)
                y_ref[pl.ds(r0, q), :] = (yt * lax.rsqrt(ms + RMS_EPS) * nrm_ref[...]).astype(y_ref.dtype)
            return carry

        lax.fori_loop(0, nc, chunk, 0)


def _ssd(xbc, proj, psmall, alog, dtb, dskip, nrm, batch, d_ssd, n_state):
    t = xbc.shape[0]
    seq = t // batch
    g = SSD_GROUPS
    hp = d_ssd // g
    nb0 = d_ssd // n_state
    return pl.pallas_call(
        _ssd_body,
        grid=(batch, g),
        in_specs=[
            pl.BlockSpec((seq, hp), lambda b, j: (b, j)),
            pl.BlockSpec((seq, n_state), lambda b, j: (b, nb0 + j)),
            pl.BlockSpec((seq, n_state), lambda b, j: (b, nb0 + g + j)),
            pl.BlockSpec((seq, hp), lambda b, j: (b, j)),
            pl.BlockSpec((seq, LANES), lambda b, j: (b, j)),
            pl.BlockSpec((1, 1, LANES), lambda b, j: (j, 0, 0)),
            pl.BlockSpec((1, 1, LANES), lambda b, j: (j, 0, 0)),
            pl.BlockSpec((1, hp), lambda b, j: (0, j)),
            pl.BlockSpec((1, hp), lambda b, j: (0, j)),
        ],
        out_specs=pl.BlockSpec((seq, hp), lambda b, j: (b, j)),
        out_shape=jax.ShapeDtypeStruct((t, d_ssd), BF16),
        scratch_shapes=[pltpu.VMEM((seq, hp), F32), pltpu.VMEM((n_state, hp), F32)],
        compiler_params=_params("parallel", "parallel"),
        name="ssd_scan",
    )(xbc, xbc, xbc, proj, psmall, alog, dtb, dskip, nrm)


def _gla_body(q_ref, k_ref, v_ref, go_ref, ps_ref, w2_ref, gkb_ref, nrm_ref, o_ref, oacc_ref, s_ref):
    seq, dk = q_ref.shape
    dv = v_ref.shape[1]
    cq = GLA_CHUNK
    nc = seq // cq
    scale = dk ** -0.5
    row = lax.broadcasted_iota(I32, (cq, cq), 0)
    col = lax.broadcasted_iota(I32, (cq, cq), 1)
    lower = (col <= row).astype(F32)
    upper = (col >= row).astype(F32)
    ones = jnp.ones((cq, LANES), F32)

    for dirn in (0, 1):
        s_ref[...] = jnp.zeros_like(s_ref)
        tri = lower if dirn == 0 else upper
        mask = (col <= row) if dirn == 0 else (col >= row)
        w2 = w2_ref[0, dirn * LANES:(dirn + 1) * LANES, :].astype(BF16)
        gb = gkb_ref[0, dirn:dirn + 1, :]
        edge = cq - 1 if dirn == 0 else 0

        def chunk(ci, carry, dirn=dirn, tri=tri, mask=mask, w2=w2, gb=gb, edge=edge):
            c = ci if dirn == 0 else nc - 1 - ci
            r0 = pl.multiple_of(c * cq, cq)
            lr = ps_ref[pl.ds(r0, cq), :].astype(BF16)
            gl = _dot(lr, w2) + gb
            g = -_softplus(-gl) * (1.0 / GLA_NORMALIZER)
            bcum = jnp.dot(tri, g, precision=HI, preferred_element_type=F32)
            blast = bcum[edge:edge + 1, :]
            qf = q_ref[pl.ds(r0, cq), :].astype(F32) * scale
            kf = k_ref[pl.ds(r0, cq), :].astype(F32)
            v = v_ref[pl.ds(r0, cq), :]
            q_in = (qf * jnp.exp(bcum)).astype(BF16)
            k_in = (kf * jnp.exp(-bcum)).astype(BF16)
            k_st = (kf * jnp.exp(blast - bcum)).astype(BF16)
            att = jnp.where(mask, _dot_nt(q_in, k_in), 0.0).astype(BF16)
            st = s_ref[...]
            o = _dot(att, v) + _dot(q_in, st.astype(BF16))
            dcol = jnp.exp(lax.dot_general(g, ones, TN_DIMS, precision=HI, preferred_element_type=F32))
            if dv > LANES:
                dfull = jnp.concatenate([dcol] * (dv // LANES), axis=1)
            else:
                dfull = dcol[:, :dv]
            s_ref[...] = st * dfull + _dot_tn(k_st, v)
            if dirn == 0:
                oacc_ref[pl.ds(r0, cq), :] = o
            else:
                ot = oacc_ref[pl.ds(r0, cq), :] + o
                ms = jnp.mean(ot * ot, axis=-1, keepdims=True)
                gf = go_ref[pl.ds(r0, cq), :].astype(F32)
                o_ref[pl.ds(r0, cq), :] = (ot * lax.rsqrt(ms + RMS_EPS) * nrm_ref[...] * (gf * _sigmoid(gf))).astype(o_ref.dtype)
            return carry

        lax.fori_loop(0, nc, chunk, 0)


def _gla(proj, psmall, w2pad, gkb, nrm, batch, q0, k0, v0, g0, gk_blk, dk, dv):
    t = proj.shape[0]
    seq = t // batch
    h = GLA_HEADS
    return pl.pallas_call(
        _gla_body,
        grid=(batch, h),
        in_specs=[
            pl.BlockSpec((seq, dk), lambda b, j: (b, q0 // dk + j)),
            pl.BlockSpec((seq, dk), lambda b, j: (b, k0 // dk + j)),
            pl.BlockSpec((seq, dv), lambda b, j: (b, v0 // dv + j)),
            pl.BlockSpec((seq, dv), lambda b, j: (b, g0 // dv + j)),
            pl.BlockSpec((seq, LANES), lambda b, j: (b, gk_blk)),
            pl.BlockSpec((1, 2 * LANES, dk), lambda b, j: (j, 0, 0)),
            pl.BlockSpec((1, 2, dk), lambda b, j: (j, 0, 0)),
            pl.BlockSpec((1, dv), lambda b, j: (0, 0)),
        ],
        out_specs=pl.BlockSpec((seq, dv), lambda b, j: (b, j)),
        out_shape=jax.ShapeDtypeStruct((t, h * dv), BF16),
        scratch_shapes=[pltpu.VMEM((seq, dv), F32), pltpu.VMEM((dk, dv), F32)],
        compiler_params=_params("parallel", "parallel"),
        name="gla_scan",
    )(proj, proj, proj, proj, psmall, w2pad, gkb, nrm)


def _ln_epilogue(h, xres_ref, g_ref, b_ref, alpha, outs):
    v = _layer_norm(alpha * xres_ref[...] + h, g_ref[...], b_ref[...])
    of_ref, ob_ref, opk_ref = outs
    of_ref[...] = v
    if ob_ref is not None:
        ob_ref[...] = v.astype(BF16)
    if opk_ref is not None:
        opk_ref[...] = _pack_rows(v)


def _even_out_body(y_ref, o_ref, w_ref, xres_ref, g_ref, b_ref, of_ref, ob_ref, acc_ref, *, nk1, alpha):
    k = pl.program_id(1)

    @pl.when(k == 0)
    def _():
        acc_ref[...] = jnp.zeros_like(acc_ref)

    @pl.when(k < nk1)
    def _():
        acc_ref[...] += _dot(y_ref[...], w_ref[...])

    @pl.when(k >= nk1)
    def _():
        acc_ref[...] += _dot(o_ref[...], w_ref[...])

    @pl.when(k == pl.num_programs(1) - 1)
    def _():
        _ln_epilogue(acc_ref[...], xres_ref, g_ref, b_ref, alpha, (of_ref, ob_ref, None))


def _even_out(y, o, w, xres, g, b, alpha, tm=512, tk=1024):
    t, k1 = y.shape
    k2 = o.shape[1]
    d = w.shape[1]
    tm = min(tm, t)
    tk = min(tk, k1, k2)
    assert k1 % tk == 0 and k2 % tk == 0 and t % tm == 0
    nk1, nk2 = k1 // tk, k2 // tk
    return pl.pallas_call(
        functools.partial(_even_out_body, nk1=nk1, alpha=alpha),
        grid=(t // tm, nk1 + nk2),
        in_specs=[
            pl.BlockSpec((tm, tk), lambda i, k: (i, jnp.minimum(k, nk1 - 1))),
            pl.BlockSpec((tm, tk), lambda i, k: (i, jnp.maximum(k - nk1, 0))),
            pl.BlockSpec((tk, d), lambda i, k: (k, 0)),
            pl.BlockSpec((tm, d), lambda i, k: (i, 0)),
            pl.BlockSpec((1, d), lambda i, k: (0, 0)),
            pl.BlockSpec((1, d), lambda i, k: (0, 0)),
        ],
        out_specs=[pl.BlockSpec((tm, d), lambda i, k: (i, 0)), pl.BlockSpec((tm, d), lambda i, k: (i, 0))],
        out_shape=[jax.ShapeDtypeStruct((t, d), F32), jax.ShapeDtypeStruct((t, d), BF16)],
        scratch_shapes=[pltpu.VMEM((tm, d), F32)],
        compiler_params=_params("parallel", "arbitrary"),
        name="even_out_ln",
    )(y, o, w, xres, g, b)


def _gelu_tanh(x):
    return 0.5 * x * (1.0 + jnp.tanh(0.7978845608028654 * (x + 0.044715 * x * x * x)))


def _odd_out_body(gate_ref, hf_ref, hb_ref, w_ref, xres_ref, g_ref, b_ref, of_ref, ob_ref, acc_ref, *, alpha):
    k = pl.program_id(1)

    @pl.when(k == 0)
    def _():
        acc_ref[...] = jnp.zeros_like(acc_ref)

    hsum = hf_ref[...].astype(F32) + hb_ref[...].astype(F32)
    a = (_gelu_tanh(gate_ref[...].astype(F32)) * hsum).astype(BF16)
    acc_ref[...] += _dot(a, w_ref[...])

    @pl.when(k == pl.num_programs(1) - 1)
    def _():
        _ln_epilogue(acc_ref[...], xres_ref, g_ref, b_ref, alpha, (of_ref, ob_ref, None))


def _odd_out(proj, hf, hb, w, xres, g, b, alpha, tm=512):
    t, kw = hf.shape
    d = w.shape[1]
    tm = min(tm, t)
    nk = 2 if (kw // 2) % LANES == 0 else 1
    tk = kw // nk
    return pl.pallas_call(
        functools.partial(_odd_out_body, alpha=alpha),
        grid=(t // tm, nk),
        in_specs=[
            pl.BlockSpec((tm, tk), lambda i, k: (i, k)),
            pl.BlockSpec((tm, tk), lambda i, k: (i, k)),
            pl.BlockSpec((tm, tk), lambda i, k: (i, k)),
            pl.BlockSpec((tk, d), lambda i, k: (k, 0)),
            pl.BlockSpec((tm, d), lambda i, k: (i, 0)),
            pl.BlockSpec((1, d), lambda i, k: (0, 0)),
            pl.BlockSpec((1, d), lambda i, k: (0, 0)),
        ],
        out_specs=[pl.BlockSpec((tm, d), lambda i, k: (i, 0)), pl.BlockSpec((tm, d), lambda i, k: (i, 0))],
        out_shape=[jax.ShapeDtypeStruct((t, d), F32), jax.ShapeDtypeStruct((t, d), BF16)],
        scratch_shapes=[pltpu.VMEM((tm, d), F32)],
        compiler_params=_params("parallel", "arbitrary"),
        name="odd_out_ln",
    )(proj, hf, hb, w, xres, g, b)


def _rglru_body(vf_ref, vb_ref, wa_ref, wx_ref, ba_ref, bx_ref, lam_ref, hf_ref, hb_ref, af_ref, uf_ref, ab_ref, ub_ref, hc_ref):
    tt, width = vf_ref.shape
    nb, bw = wa_ref.shape[1], wa_ref.shape[2]

    @pl.when(pl.program_id(1) == 0)
    def _():
        hc_ref[...] = jnp.zeros_like(hc_ref)

    sp = _softplus(-lam_ref[...])
    for d, (v_ref, a_s, u_s) in enumerate(((vf_ref, af_ref, uf_ref), (vb_ref, ab_ref, ub_ref))):
        for n in range(nb):
            cs = slice(n * bw, (n + 1) * bw)
            vb16 = v_ref[:, cs]
            r = _sigmoid(_dot(vb16, wa_ref[d, n]) + ba_ref[d:d + 1, cs])
            ig = _sigmoid(_dot(vb16, wx_ref[d, n]) + bx_ref[d:d + 1, cs])
            a = jnp.exp(-RG_C * r * sp[d:d + 1, cs])
            a_s[:, cs] = a
            u_s[:, cs] = jnp.sqrt(1.0 - a * a) * (ig * vb16.astype(F32))

    def step(t8, carry):
        hf, hb = carry
        for s in range(8):
            tf = t8 * 8 + s
            tb = tt - 1 - tf
            hf = af_ref[pl.ds(tf, 1), :] * hf + uf_ref[pl.ds(tf, 1), :]
            uf_ref[pl.ds(tf, 1), :] = hf
            hb = ab_ref[pl.ds(tb, 1), :] * hb + ub_ref[pl.ds(tb, 1), :]
            ub_ref[pl.ds(tb, 1), :] = hb
        return hf, hb

    hf, hb = lax.fori_loop(0, tt // 8, step, (hc_ref[0:1, :], hc_ref[1:2, :]))
    hc_ref[0:1, :] = hf
    hc_ref[1:2, :] = hb
    hf_ref[...] = uf_ref[...].astype(hf_ref.dtype)
    hb_ref[...] = ub_ref[...].astype(hb_ref.dtype)


def _rglru(vc, wa, wx, ba, bx, lam, batch, tt=256):
    t, width = vc.shape
    seq = t // batch
    tt = min(tt, seq)
    ntt = seq // tt
    nb, bw = wa.shape[1], wa.shape[2]
    full = lambda *shape: pl.BlockSpec(shape, lambda b, j: (0,) * len(shape))
    return pl.pallas_call(
        _rglru_body,
        grid=(batch, ntt),
        in_specs=[
            pl.BlockSpec((tt, width), lambda b, j: (b * ntt + j, 0)),
            pl.BlockSpec((tt, width), lambda b, j: (b * ntt + ntt - 1 - j, 0)),
            full(2, nb, bw, bw),
            full(2, nb, bw, bw),
            full(2, width),
            full(2, width),
            full(2, width),
        ],
        out_specs=[
            pl.BlockSpec((tt, width), lambda b, j: (b * ntt + j, 0)),
            pl.BlockSpec((tt, width), lambda b, j: (b * ntt + ntt - 1 - j, 0)),
        ],
        out_shape=[jax.ShapeDtypeStruct((t, width), BF16), jax.ShapeDtypeStruct((t, width), BF16)],
        scratch_shapes=[pltpu.VMEM((tt, width), F32) for _ in range(4)] + [pltpu.VMEM((8, width), F32)],
        compiler_params=_params("parallel", "arbitrary"),
        name="rglru",
    )(vc, vc, wa, wx, ba, bx, lam)


def _xattn_body(q_ref, k_ref, v_ref, wo_ref, xres_ref, g_ref, b_ref, of_ref, opk_ref, *, alpha):
    tm, d = q_ref.shape
    hd = d // XATTN_HEADS
    scale = hd ** -0.5
    acc = jnp.zeros((tm, d), F32)
    for h in range(XATTN_HEADS):
        cs = slice(h * hd, (h + 1) * hd)
        s = _dot_nt(q_ref[:, cs], k_ref[:, cs]) * scale
        s = s - jnp.max(s, axis=-1, keepdims=True)
        e = jnp.exp(s)
        pr = (e / jnp.sum(e, axis=-1, keepdims=True)).astype(BF16)
        oh = _dot(pr, v_ref[:, cs]).astype(BF16)
        acc = acc + _dot(oh, wo_ref[cs, :])
    _ln_epilogue(acc, xres_ref, g_ref, b_ref, alpha, (of_ref, None, opk_ref))


def _xattn(q, kv, wo, xres, g, b, alpha, batch, tm=512):
    t, d = q.shape
    seq = t // batch
    mem = kv.shape[0] // batch
    tm = min(tm, seq)
    return pl.pallas_call(
        functools.partial(_xattn_body, alpha=alpha),
        grid=(t // tm,),
        in_specs=[
            pl.BlockSpec((tm, d), lambda i: (i, 0)),
            pl.BlockSpec((mem, d), lambda i: (i * tm // seq, 0)),
            pl.BlockSpec((mem, d), lambda i: (i * tm // seq, 1)),
            pl.BlockSpec((d, d), lambda i: (0, 0)),
            pl.BlockSpec((tm, d), lambda i: (i, 0)),
            pl.BlockSpec((1, d), lambda i: (0, 0)),
            pl.BlockSpec((1, d), lambda i: (0, 0)),
        ],
        out_specs=[pl.BlockSpec((tm, d), lambda i: (i, 0)), pl.BlockSpec((tm, d // 2), lambda i: (i, 0))],
        out_shape=[jax.ShapeDtypeStruct((t, d), F32), jax.ShapeDtypeStruct((t, d // 2), I32)],
        compiler_params=_params("parallel"),
        name="xattn_out_ln",
    )(q, kv, kv, wo, xres, g, b)


def _router_body(x_ref, w_ref, b_ref, ri_ref, rg_ref, cnt_ref, carry_ref):
    tr = x_ref.shape[0]

    @pl.when(pl.program_id(0) == 0)
    def _():
        carry_ref[...] = jnp.zeros_like(carry_ref)

    logits = jnp.dot(x_ref[...], w_ref[...], precision=HI, preferred_element_type=F32) + b_ref[...]
    lane = lax.broadcasted_iota(I32, (tr, LANES), 1).astype(F32)
    work = logits
    sels, vals, idxs = [], [], []
    for _ in range(TOP_K):
        m = jnp.max(work, axis=-1, keepdims=True)
        idx = jnp.min(jnp.where(work == m, lane, float(LANES)), axis=-1, keepdims=True)
        sel = lane == idx
        sels.append(sel)
        vals.append(m)
        idxs.append(idx)
        work = jnp.where(sel, NEG_BIG, work)
    exps = [jnp.exp(v - vals[0]) for v in vals]
    den = exps[0]
    for e in exps[1:]:
        den = den + e
    onehot = jnp.zeros((tr, LANES), F32)
    for sel in sels:
        onehot = onehot + sel.astype(F32)
    row = lax.broadcasted_iota(I32, (tr, tr), 0)
    col = lax.broadcasted_iota(I32, (tr, tr), 1)
    strict = (col < row).astype(BF16)
    before = _dot(strict, onehot.astype(BF16)) + carry_ref[...]
    ri = jnp.zeros((tr, LANES), F32)
    rg = jnp.zeros((tr, LANES), F32)
    for k in range(TOP_K):
        rank = jnp.sum(jnp.where(sels[k], before, 0.0), axis=-1, keepdims=True)
        ri = jnp.where(lane == float(k), idxs[k], ri)
        ri = jnp.where(lane == float(TOP_K + k), rank, ri)
        rg = jnp.where(lane == float(k), exps[k] / den, rg)
    ri_ref[...] = ri.astype(I32)
    rg_ref[...] = rg
    total = carry_ref[...] + jnp.sum(onehot, axis=0, keepdims=True)
    carry_ref[...] = total
    cnt_ref[...] = total


def _router(x, w_pad, b_pad, tr=512):
    t, d = x.shape
    tr = min(tr, t)
    return pl.pallas_call(
        _router_body,
        grid=(t // tr,),
        in_specs=[
            pl.BlockSpec((tr, d), lambda i: (i, 0)),
            pl.BlockSpec((d, LANES), lambda i: (0, 0)),
            pl.BlockSpec((1, LANES), lambda i: (0, 0)),
        ],
        out_specs=[
            pl.BlockSpec((tr, LANES), lambda i: (i, 0)),
            pl.BlockSpec((tr, LANES), lambda i: (i, 0)),
            pl.BlockSpec((1, LANES), lambda i: (0, 0)),
        ],
        out_shape=[
            jax.ShapeDtypeStruct((t, LANES), I32),
            jax.ShapeDtypeStruct((t, LANES), F32),
            jax.ShapeDtypeStruct((1, LANES), F32),
        ],
        scratch_shapes=[pltpu.VMEM((1, LANES), F32)],
        compiler_params=_params("arbitrary"),
        name="moe_router",
    )(x, w_pad, b_pad)


def _dispatch_body(pos_ref, xpk_ref, xs_in_ref, xs_ref, sem):
    del xs_in_ref
    n = pos_ref.shape[-1]
    base = pl.program_id(0) * (n // TOP_K)

    def row_copy(j):
        return pltpu.make_async_copy(xpk_ref.at[pl.ds(base + j // TOP_K, 1)], xs_ref.at[pl.ds(pos_ref[0, 0, j], 1)], sem)

    def issue(j, carry):
        row_copy(j).start()
        return carry

    def drain(j, carry):
        row_copy(j).wait()
        return carry

    lax.fori_loop(0, n, issue, 0)
    lax.fori_loop(0, n, drain, 0)


def _dispatch(xpk, pos, n_rows, ts=256):
    t, dh = xpk.shape
    ts = min(ts, t)
    pos3 = pos.reshape(t // ts, 1, ts * TOP_K)
    xs0 = jnp.zeros((n_rows, dh), I32)
    return pl.pallas_call(
        _dispatch_body,
        grid=(t // ts,),
        in_specs=[
            pl.BlockSpec((1, 1, ts * TOP_K), lambda i: (i, 0, 0), memory_space=pltpu.SMEM),
            pl.BlockSpec(memory_space=pl.ANY),
            pl.BlockSpec(memory_space=pl.ANY),
        ],
        out_specs=pl.BlockSpec(memory_space=pl.ANY),
        out_shape=jax.ShapeDtypeStruct((n_rows, dh), I32),
        scratch_shapes=[pltpu.SemaphoreType.DMA(())],
        input_output_aliases={2: 0},
        compiler_params=_params("arbitrary"),
        name="moe_dispatch",
    )(pos3, xpk, xs0)


def _moe_body(te_ref, tr_ref, nu_ref, xs_ref, wg_ref, wl_ref, bg_ref, bl_ref, wd_ref, bd_ref, o_ref,
              acc_ref, wgb_ref, wlb_ref, wdb_ref, *, sub):
    del te_ref
    i = pl.program_id(0)
    f = pl.program_id(1)
    tm, dh = xs_ref.shape

    @pl.when(i < nu_ref[0])
    def _():
        wgb_ref[...] = wg_ref[0].astype(BF16)
        wlb_ref[...] = wl_ref[0].astype(BF16)
        wdb_ref[...] = wd_ref[0].astype(BF16)

        @pl.when(f == 0)
        def _():
            acc_ref[...] = jnp.broadcast_to(bd_ref[0], acc_ref.shape)

        nsub = (tr_ref[i] + sub - 1) // sub

        def body(s, carry):
            r0 = pl.multiple_of(s * sub, sub)
            lo, hi = _unpack_rows(xs_ref[pl.ds(r0, sub), :])
            lo = lo.astype(BF16)
            hi = hi.astype(BF16)
            hg = _dot(lo, wgb_ref[0:dh, :]) + _dot(hi, wgb_ref[dh:2 * dh, :]) + bg_ref[0]
            hl = _dot(lo, wlb_ref[0:dh, :]) + _dot(hi, wlb_ref[dh:2 * dh, :]) + bl_ref[0]
            glu = jnp.minimum(hg, SWIGLU_LIMIT)
            lin = jnp.clip(hl, -SWIGLU_LIMIT, SWIGLU_LIMIT)
            act = ((lin + 1.0) * glu * _sigmoid(SWIGLU_ALPHA * glu)).astype(BF16)
            acc_ref[pl.ds(r0, sub), :] += _dot(act, wdb_ref[...])
            return carry

        lax.fori_loop(0, nsub, body, 0)

        @pl.when(f == pl.num_programs(1) - 1)
        def _():
            for s in range(tm // sub):
                o_ref[s * sub:(s + 1) * sub, :] = _pack_rows(acc_ref[s * sub:(s + 1) * sub, :])


def _moe_experts(xs, w_gu, b_gu, w_dn, b_dn, tile_e, tile_rows, n_used, tm, tf, sub):
    n_rows, dh = xs.shape
    d = 2 * dh
    e, _, ff2 = w_gu.shape
    ff = ff2 // 2
    tf = min(tf, ff)
    nf = ff // tf
    nt = n_rows // tm

    def tile_idx(i, f, te, tr, nu):
        return (jnp.minimum(i, nu[0] - 1), 0)

    def fidx(i, f, nu):
        return jnp.where(i < nu[0], f, nf - 1)

    grid_spec = pltpu.PrefetchScalarGridSpec(
        num_scalar_prefetch=3,
        grid=(nt, nf),
        in_specs=[
            pl.BlockSpec((tm, dh), tile_idx),
            pl.BlockSpec((1, d, tf), lambda i, f, te, tr, nu: (te[i], 0, fidx(i, f, nu))),
            pl.BlockSpec((1, d, tf), lambda i, f, te, tr, nu: (te[i], 0, nf + fidx(i, f, nu))),
            pl.BlockSpec((1, 1, tf), lambda i, f, te, tr, nu: (te[i], 0, fidx(i, f, nu))),
            pl.BlockSpec((1, 1, tf), lambda i, f, te, tr, nu: (te[i], 0, nf + fidx(i, f, nu))),
            pl.BlockSpec((1, tf, d), lambda i, f, te, tr, nu: (te[i], fidx(i, f, nu), 0)),
            pl.BlockSpec((1, 1, d), lambda i, f, te, tr, nu: (te[i], 0, 0)),
        ],
        out_specs=pl.BlockSpec((tm, dh), tile_idx),
        scratch_shapes=[
            pltpu.VMEM((tm, d), F32),
            pltpu.VMEM((d, tf), BF16),
            pltpu.VMEM((d, tf), BF16),
            pltpu.VMEM((tf, d), BF16),
        ],
    )
    return pl.pallas_call(
        functools.partial(_moe_body, sub=sub),
        grid_spec=grid_spec,
        out_shape=jax.ShapeDtypeStruct((n_rows, dh), I32),
        input_output_aliases={3: 0},
        compiler_params=_params("arbitrary", "arbitrary"),
        name="moe_experts",
    )(tile_e, tile_rows, n_used, xs, w_gu, w_gu, b_gu.reshape(e, 1, ff2), b_gu.reshape(e, 1, ff2), w_dn,
      b_dn.reshape(e, 1, d))


def _combine_body(pos_ref, rg_ref, outs_ref, xres_ref, g_ref, b_ref, of_ref, ob_ref, buf_ref, sem, *, alpha):
    n = pos_ref.shape[-1]

    def row_copy(j):
        return pltpu.make_async_copy(outs_ref.at[pl.ds(pos_ref[0, 0, j], 1)],
                                     buf_ref.at[j % TOP_K, pl.ds(j // TOP_K, 1)], sem)

    def issue(j, carry):
        row_copy(j).start()
        return carry

    def drain(j, carry):
        row_copy(j).wait()
        return carry

    lax.fori_loop(0, n, issue, 0)
    lax.fori_loop(0, n, drain, 0)
    gates = rg_ref[...]
    ylo = yhi = None
    for k in range(TOP_K):
        lo, hi = _unpack_rows(buf_ref[k])
        gk = gates[:, k:k + 1]
        ylo = gk * lo if ylo is None else ylo + gk * lo
        yhi = gk * hi if yhi is None else yhi + gk * hi
    y = jnp.concatenate([ylo, yhi], axis=1)
    _ln_epilogue(y, xres_ref, g_ref, b_ref, alpha, (of_ref, ob_ref, None))


def _combine(outs, pos, rg, xres, g, b, alpha, tc=128):
    t, d = xres.shape
    tc = min(tc, t)
    dh = d // 2
    pos3 = pos.reshape(t // tc, 1, tc * TOP_K)
    return pl.pallas_call(
        functools.partial(_combine_body, alpha=alpha),
        grid=(t // tc,),
        in_specs=[
            pl.BlockSpec((1, 1, tc * TOP_K), lambda i: (i, 0, 0), memory_space=pltpu.SMEM),
            pl.BlockSpec((tc, LANES), lambda i: (i, 0)),
            pl.BlockSpec(memory_space=pl.ANY),
            pl.BlockSpec((tc, d), lambda i: (i, 0)),
            pl.BlockSpec((1, d), lambda i: (0, 0)),
            pl.BlockSpec((1, d), lambda i: (0, 0)),
        ],
        out_specs=[pl.BlockSpec((tc, d), lambda i: (i, 0)), pl.BlockSpec((tc, d), lambda i: (i, 0))],
        out_shape=[jax.ShapeDtypeStruct((t, d), F32), jax.ShapeDtypeStruct((t, d), BF16)],
        scratch_shapes=[pltpu.VMEM((TOP_K, tc, dh), I32), pltpu.SemaphoreType.DMA(())],
        compiler_params=_params("arbitrary"),
        name="moe_combine_ln",
    )(pos3, rg, outs, xres, g, b)


def _moe_layer(xf, xpk, router_w, router_b, w_gu, b_gu, w_dn, b_dn, g, b, alpha, tm, tf, sub):
    t, d = xf.shape
    e = router_w.shape[1]
    w_pad = jnp.zeros((d, LANES), F32).at[:, :e].set(router_w)
    b_pad = jnp.full((1, LANES), -1e30, F32).at[0, :e].set(router_b)
    ri, rg, cnt = _router(xf, w_pad, b_pad)
    counts = cnt[0, :e].astype(I32)
    nt_e = (counts + tm - 1) // tm
    tile_end = jnp.cumsum(nt_e)
    tile_start = tile_end - nt_e
    n_used = tile_end[-1]
    nt = (t * TOP_K) // tm + e
    pos = (tile_start * tm)[ri[:, 0:TOP_K]] + ri[:, TOP_K:2 * TOP_K]
    tid = jnp.minimum(jnp.arange(nt, dtype=I32), n_used - 1)
    tile_e = jnp.minimum(jnp.searchsorted(tile_end, tid, side="right"), e - 1).astype(I32)
    tile_rows = jnp.clip(counts[tile_e] - (tid - tile_start[tile_e]) * tm, 0, tm).astype(I32)
    xs = _dispatch(xpk, pos, nt * tm)
    outs = _moe_experts(xs, w_gu, b_gu, w_dn, b_dn, tile_e, tile_rows, n_used.reshape(1).astype(I32), tm, tf, sub)
    return _combine(outs, pos, rg, xf, g, b, alpha)


def _xattn_layer(xf, xb, mem_b, wq, wkv, wo, g, b, alpha, batch):
    q = _matmul(xb, wq.astype(BF16), BF16)
    kv = _matmul(mem_b, wkv.astype(BF16), BF16)
    return _xattn(q, kv, wo.astype(BF16), xf, g, b, alpha, batch)


def _even_mixer(xf, xb, w_in, conv_w, conv_b, a_log, dt_bias, d_skip, ssd_norm, gk_w2, gk_b, gla_norm, w_out, g, b,
                alpha, batch):
    d = xf.shape[1]
    grp = SSD_GROUPS
    heads = a_log.shape[1]
    r_heads = heads // grp
    d_ssd = heads * SSD_HEAD_DIM
    n_state = (conv_w.shape[1] - d_ssd) // (2 * grp)
    conv_ch = d_ssd + 2 * grp * n_state
    rank = gk_w2.shape[1]
    dk = gk_w2.shape[2] // GLA_HEADS
    dv = gla_norm.shape[0]
    o_xbc = d_ssd
    o_dt = o_xbc + conv_ch
    o_q = o_dt + 2 * heads
    o_k = o_q + GLA_HEADS * dk
    o_v = o_k + GLA_HEADS * dk
    o_g = o_v + GLA_HEADS * dv
    o_lr = o_g + GLA_HEADS * dv
    assert 2 * r_heads <= LANES and 2 * rank <= LANES and o_lr + 2 * rank == w_in.shape[1]
    w_big = jnp.concatenate([w_in[:, :o_dt], w_in[:, o_q:o_lr]], axis=1).astype(BF16)
    w_dt = w_in[:, o_dt:o_q].reshape(d, 2, grp, r_heads).transpose(0, 2, 1, 3).reshape(d, grp, 2 * r_heads)
    w_dt = jnp.pad(w_dt, ((0, 0), (0, 0), (0, LANES - 2 * r_heads))).reshape(d, grp * LANES)
    w_lr = jnp.pad(w_in[:, o_lr:], ((0, 0), (0, LANES - 2 * rank)))
    w_small = jnp.concatenate([w_dt, w_lr], axis=1).astype(BF16)
    proj = _matmul(xb, w_big, BF16)
    psmall = _matmul(xb, w_small, F32)

    xbc = _dwconv(proj, o_xbc, conv_ch, conv_w, conv_b, batch, act=True)

    def group_lanes(p):
        p = p.reshape(2, grp, r_heads).transpose(1, 0, 2).reshape(grp, 1, 2 * r_heads)
        return jnp.pad(p, ((0, 0), (0, 0), (0, LANES - 2 * r_heads)))

    y = _ssd(xbc, proj, psmall, group_lanes(a_log), group_lanes(dt_bias),
             jnp.repeat(d_skip, SSD_HEAD_DIM).reshape(1, d_ssd), ssd_norm.reshape(1, d_ssd), batch, d_ssd, n_state)

    w2 = gk_w2.reshape(2, rank, GLA_HEADS, dk).transpose(2, 0, 1, 3)
    w2pad = jnp.zeros((GLA_HEADS, 2, LANES, dk), F32)
    for dirn in range(2):
        w2pad = w2pad.at[:, dirn, dirn * rank:(dirn + 1) * rank, :].set(w2[:, dirn])
    w2pad = w2pad.reshape(GLA_HEADS, 2 * LANES, dk)
    gkb = gk_b.reshape(2, GLA_HEADS, dk).transpose(1, 0, 2)
    q0 = o_dt
    k0 = q0 + GLA_HEADS * dk
    v0 = k0 + GLA_HEADS * dk
    g0 = v0 + GLA_HEADS * dv
    o = _gla(proj, psmall, w2pad, gkb, gla_norm.reshape(1, dv), batch, q0, k0, v0, g0, grp, dk, dv)
    return _even_out(y, o, w_out.astype(BF16), xf, g, b, alpha)


def _odd_mixer(xf, xb, w_in, conv_w, conv_b, wa, ba, wx, bx, lam, w_out, g, b, alpha, batch):
    width = conv_w.shape[1]
    proj = _matmul(xb, w_in.astype(BF16), BF16)
    vc = _dwconv(proj, width, width, conv_w, conv_b, batch, act=False)
    hf, hb = _rglru(vc, wa.astype(BF16), wx.astype(BF16), ba, bx, lam, batch)
    return _odd_out(proj, hf, hb, w_out.astype(BF16), xf, g, b, alpha)


def kernel(x, mem, ln_g, ln_b, even_w_in, even_conv_w, even_conv_b, ssd_a_log, ssd_dt_bias, ssd_d, ssd_norm, gla_gk_w2, gla_gk_b, gla_norm, even_w_out, odd_w_in, odd_conv_w, odd_conv_b, rg_wa, rg_ba, rg_wx, rg_bx, rg_lam, odd_w_out, xattn_wq, xattn_wkv, xattn_wo, router_w, router_b, moe_w_gu, moe_b_gu, moe_w_dn, moe_b_dn):
    batch, seq, d = x.shape
    depth = ln_g.shape[0]
    alpha = float((2 * depth) ** 0.25)
    xf = x.reshape(batch * seq, d)
    xb = xf.astype(BF16)
    mem_b = mem.reshape(-1, d).astype(BF16)
    for layer in range(depth):
        i = layer // 2
        lg = lambda s: ln_g[layer, s].reshape(1, d)
        lb = lambda s: ln_b[layer, s].reshape(1, d)
        if layer % 2 == 0:
            xf, xb = _even_mixer(xf, xb, even_w_in[i], even_conv_w[i], even_conv_b[i], ssd_a_log[i], ssd_dt_bias[i],
                                 ssd_d[i], ssd_norm[i], gla_gk_w2[i], gla_gk_b[i], gla_norm[i], even_w_out[i],
                                 lg(0), lb(0), alpha, batch)
        else:
            xf, xb = _odd_mixer(xf, xb, odd_w_in[i], odd_conv_w[i], odd_conv_b[i], rg_wa[i], rg_ba[i], rg_wx[i],
                                rg_bx[i], rg_lam[i], odd_w_out[i], lg(0), lb(0), alpha, batch)
        xf, xpk = _xattn_layer(xf, xb, mem_b, xattn_wq[layer], xattn_wkv[layer], xattn_wo[layer], lg(1), lb(1), alpha,
                               batch)
        xf, xb = _moe_layer(xf, xpk, router_w[layer], router_b[layer], moe_w_gu[layer], moe_b_gu[layer],
                            moe_w_dn[layer], moe_b_dn[layer], lg(2), lb(2), alpha, MOE_TILE, MOE_FF_TILE, MOE_SUB)
    return xf.reshape(batch, seq, d)
```

```python
import functools

import jax
import jax.numpy as jnp
from jax import lax
from jax.experimental import pallas as pl
from jax.experimental.pallas import tpu as pltpu

F32 = jnp.float32
BF16 = jnp.bfloat16
I32 = jnp.int32

SSD_HEAD_DIM = 64
SSD_GROUPS = 4
SSD_CHUNK = 128
GLA_HEADS = 4
GLA_NORMALIZER = 16.0
GLA_CHUNK = 64
RG_C = 8.0
XATTN_HEADS = 4
TOP_K = 4
SWIGLU_ALPHA = 1.702
SWIGLU_LIMIT = 7.0
LN_EPS = 1e-5
RMS_EPS = 1e-6

LANES = 128
VMEM_LIMIT_BYTES = 56 * 1024 * 1024
CONV_HALO = 16
MOE_TILE = 1024
MOE_FF_TILE = 512
MOE_SUB = 256
NEG_BIG = -3.0e38
HI = lax.Precision.HIGHEST

TN_DIMS = (((0,), (0,)), ((), ()))
NT_DIMS = (((1,), (1,)), ((), ()))


def _params(*sem):
    return pltpu.CompilerParams(dimension_semantics=sem, vmem_limit_bytes=VMEM_LIMIT_BYTES)


def _dot(a, b):
    return jnp.dot(a, b, preferred_element_type=F32)


def _dot_nt(a, b):
    return lax.dot_general(a, b, NT_DIMS, preferred_element_type=F32)


def _dot_tn(a, b):
    return lax.dot_general(a, b, TN_DIMS, preferred_element_type=F32)


def _sigmoid(x):
    return 1.0 / (1.0 + jnp.exp(-x))


def _softplus(x):
    return jnp.maximum(x, 0.0) + jnp.log(1.0 + jnp.exp(-jnp.abs(x)))


def _layer_norm(v, g, b):
    mu = jnp.mean(v, axis=-1, keepdims=True)
    c = v - mu
    var = jnp.mean(c * c, axis=-1, keepdims=True)
    return c * lax.rsqrt(var + LN_EPS) * g + b


def _pack_rows(v):
    h = v.shape[1] // 2
    lo = pltpu.bitcast(v[:, :h].astype(BF16).astype(F32), I32)
    hi = pltpu.bitcast(v[:, h:].astype(BF16).astype(F32), I32)
    return (lo & jnp.int32(-65536)) | (lax.shift_right_logical(hi, jnp.int32(16)))


def _unpack_rows(w):
    lo = pltpu.bitcast(w & jnp.int32(-65536), F32)
    hi = pltpu.bitcast(lax.shift_left(w, jnp.int32(16)), F32)
    return lo, hi


def _largest_tile(n, cap, *also):
    for c in range(min(cap, n) // LANES * LANES, 0, -LANES):
        if n % c == 0 and all(a % c == 0 for a in also):
            return c
    return n


def _mm_body(a_ref, w_ref, o_ref):
    o_ref[...] = _dot(a_ref[...], w_ref[...]).astype(o_ref.dtype)


def _matmul(a, w, out_dtype, tm=1024, tn=1024):
    m, k = a.shape
    n = w.shape[1]
    tm = min(tm, m)
    tn = _largest_tile(n, tn)
    assert m % tm == 0
    return pl.pallas_call(
        _mm_body,
        grid=(n // tn, m // tm),
        in_specs=[pl.BlockSpec((tm, k), lambda j, i: (i, 0)), pl.BlockSpec((k, tn), lambda j, i: (0, j))],
        out_specs=pl.BlockSpec((tm, tn), lambda j, i: (i, j)),
        out_shape=jax.ShapeDtypeStruct((m, n), out_dtype),
        compiler_params=_params("parallel", "parallel"),
        name="proj_matmul",
    )(a, w)


def _conv_body(x_ref, w_ref, b_ref, o_ref, *, act, rows):
    seq = x_ref.shape[0]
    nchunks = seq // rows
    tot = rows + 2 * CONV_HALO
    w = w_ref[...]
    b = b_ref[...]

    def body(i, carry):
        r0 = pl.multiple_of(i * rows, rows)
        main = x_ref[pl.ds(r0, rows), :].astype(F32)
        p0 = pl.multiple_of(jnp.maximum(r0 - CONV_HALO, 0), CONV_HALO)
        prev = x_ref[pl.ds(p0, CONV_HALO), :].astype(F32)
        prev = jnp.where(i > 0, prev, 0.0)
        n0 = pl.multiple_of(jnp.minimum(r0 + rows, seq - CONV_HALO), CONV_HALO)
        nxt = x_ref[pl.ds(n0, CONV_HALO), :].astype(F32)
        nxt = jnp.where(i < nchunks - 1, nxt, 0.0)
        ext = jnp.concatenate([prev, main, nxt], axis=0)
        acc = b + w[1:2] * main
        for j in (0, 2, 3):
            shift = (1 - j) % tot
            acc = acc + w[j:j + 1] * pltpu.roll(ext, shift, 0)[CONV_HALO:CONV_HALO + rows]
        if act:
            acc = acc * _sigmoid(acc)
        o_ref[pl.ds(r0, rows), :] = acc.astype(o_ref.dtype)
        return carry

    lax.fori_loop(0, nchunks, body, 0)


def _dwconv(x, col0, width, conv_w, conv_b, batch, act, cw=512):
    t = x.shape[0]
    seq = t // batch
    cw = _largest_tile(width, cw, col0)
    assert width % cw == 0 and col0 % cw == 0
    rows = min(256, seq)
    cb0 = col0 // cw
    return pl.pallas_call(
        functools.partial(_conv_body, act=act, rows=rows),
        grid=(batch, width // cw),
        in_specs=[
            pl.BlockSpec((seq, cw), lambda b, j: (b, cb0 + j)),
            pl.BlockSpec((conv_w.shape[0], cw), lambda b, j: (0, j)),
            pl.BlockSpec((1, cw), lambda b, j: (0, j)),
        ],
        out_specs=pl.BlockSpec((seq, cw), lambda b, j: (b, j)),
        out_shape=jax.ShapeDtypeStruct((t, width), BF16),
        compiler_params=_params("parallel", "parallel"),
        name="dwconv",
    )(x, conv_w, conv_b.reshape(1, -1))


def _ssd_body(xs_ref, b_ref, c_ref, z_ref, ps_ref, alog_ref, dtb_ref, dsk_ref, nrm_ref, y_ref, yacc_ref, h_ref):
    seq, hp = xs_ref.shape
    q = SSD_CHUNK
    nc = seq // q
    p = SSD_HEAD_DIM
    r_heads = hp // p
    a_neg = -jnp.exp(alog_ref[0])
    dtb = dtb_ref[0]
    row = lax.broadcasted_iota(I32, (q, q), 0)
    col = lax.broadcasted_iota(I32, (q, q), 1)
    lower = (col <= row).astype(F32)
    upper = (col >= row).astype(F32)
    e_row = lax.broadcasted_iota(I32, (LANES, hp), 0)
    e_col = lax.broadcasted_iota(I32, (LANES, hp), 1) // p
    pw = min(LANES, hp)
    piece_head = lax.broadcasted_iota(I32, (q, pw), 1) // p

    for dirn in (0, 1):
        h_ref[...] = jnp.zeros_like(h_ref)
        tri, tri_t = (lower, upper) if dirn == 0 else (upper, lower)
        mask = (col <= row) if dirn == 0 else (col >= row)
        expand = (e_row == e_col + dirn * r_heads).astype(BF16)
        edge = q - 1 if dirn == 0 else 0

        def chunk(ci, carry, dirn=dirn, tri=tri, tri_t=tri_t, mask=mask, expand=expand, edge=edge):
            c = ci if dirn == 0 else nc - 1 - ci
            r0 = pl.multiple_of(c * q, q)
            x = xs_ref[pl.ds(r0, q), :]
            xf = x.astype(F32)
            bm = b_ref[pl.ds(r0, q), :]
            cm = c_ref[pl.ds(r0, q), :]
            dt = _softplus(ps_ref[pl.ds(r0, q), :] + dtb)
            dta = dt * a_neg
            acum = jnp.dot(tri, dta, precision=HI, preferred_element_type=F32)
            acum_t = lax.dot_general(dta, tri_t, TN_DIMS, precision=HI, preferred_element_type=F32)
            alast = acum[edge:edge + 1, :]
            stack = jnp.concatenate([dt, jnp.exp(acum), dt * jnp.exp(alast - acum)], axis=0).astype(BF16)
            ex = _dot(stack, expand)
            xdt = (xf * ex[0:q]).astype(BF16)
            eac = ex[q:2 * q]
            xw = (xf * ex[2 * q:3 * q]).astype(BF16)
            cb = _dot_nt(cm, bm)
            yd = []
            for j in range(hp // pw):
                xpiece = xdt[:, j * pw:(j + 1) * pw]
                ypiece = None
                for a in range(pw // p):
                    cr = dirn * r_heads + j * (pw // p) + a
                    seg = acum[:, cr:cr + 1] - acum_t[cr:cr + 1, :]
                    dec = jnp.where(mask, jnp.exp(jnp.minimum(seg, 0.0)), 0.0)
                    wgt = (cb * dec).astype(BF16)
                    xa = xpiece if pw == p else jnp.where(piece_head == a, xpiece, jnp.zeros_like(xpiece))
                    ya = _dot(wgt, xa)
                    ypiece = ya if ypiece is None else ypiece + ya
                yd.append(ypiece)
            y = jnp.concatenate(yd, axis=1) if len(yd) > 1 else yd[0]
            hst = h_ref[...]
            y = y + _dot(cm, hst.astype(BF16)) * eac
            ea = jnp.broadcast_to(jnp.exp(alast), (8, LANES))
            ea_hi = ea.astype(BF16)
            ea_lo = (ea - ea_hi.astype(F32)).astype(BF16)
            hdec = (_dot(ea_hi, expand) + _dot(ea_lo, expand))[0:1]
            h_ref[...] = hst * hdec + _dot_tn(bm, xw)
            if dirn == 0:
                yacc_ref[pl.ds(r0, q), :] = y
            else:
                yt = yacc_ref[pl.ds(r0, q), :] + y + dsk_ref[...] * xf
                zf = z_ref[pl.ds(r0, q), :].astype(F32)
                yt = yt * (zf * _sigmoid(zf))
                ms = jnp.mean(yt * yt, axis=-1, keepdims=True)
                y_ref[pl.ds(r0, q), :] = (yt * lax.rsqrt(ms + RMS_EPS) * nrm_ref[...]).astype(y_ref.dtype)
            return carry

        lax.fori_loop(0, nc, chunk, 0, unroll=2)


def _ssd(xbc, proj, psmall, alog, dtb, dskip, nrm, batch, d_ssd, n_state):
    t = xbc.shape[0]
    seq = t // batch
    g = SSD_GROUPS
    hp = d_ssd // g
    nb0 = d_ssd // n_state
    return pl.pallas_call(
        _ssd_body,
        grid=(batch, g),
        in_specs=[
            pl.BlockSpec((seq, hp), lambda b, j: (b, j)),
            pl.BlockSpec((seq, n_state), lambda b, j: (b, nb0 + j)),
            pl.BlockSpec((seq, n_state), lambda b, j: (b, nb0 + g + j)),
            pl.BlockSpec((seq, hp), lambda b, j: (b, j)),
            pl.BlockSpec((seq, LANES), lambda b, j: (b, j)),
            pl.BlockSpec((1, 1, LANES), lambda b, j: (j, 0, 0)),
            pl.BlockSpec((1, 1, LANES), lambda b, j: (j, 0, 0)),
            pl.BlockSpec((1, hp), lambda b, j: (0, j)),
            pl.BlockSpec((1, hp), lambda b, j: (0, j)),
        ],
        out_specs=pl.BlockSpec((seq, hp), lambda b, j: (b, j)),
        out_shape=jax.ShapeDtypeStruct((t, d_ssd), BF16),
        scratch_shapes=[pltpu.VMEM((seq, hp), F32), pltpu.VMEM((n_state, hp), F32)],
        compiler_params=_params("parallel", "parallel"),
        name="ssd_scan",
    )(xbc, xbc, xbc, proj, psmall, alog, dtb, dskip, nrm)


def _gla_body(q_ref, k_ref, v_ref, go_ref, ps_ref, w2_ref, gkb_ref, nrm_ref, o_ref, oacc_ref, s_ref):
    seq, dk = q_ref.shape
    dv = v_ref.shape[1]
    cq = GLA_CHUNK
    nc = seq // cq
    scale = dk ** -0.5
    row = lax.broadcasted_iota(I32, (cq, cq), 0)
    col = lax.broadcasted_iota(I32, (cq, cq), 1)
    lower = (col <= row).astype(F32)
    upper = (col >= row).astype(F32)
    ones = jnp.ones((cq, LANES), F32)

    for dirn in (0, 1):
        s_ref[...] = jnp.zeros_like(s_ref)
        tri = lower if dirn == 0 else upper
        mask = (col <= row) if dirn == 0 else (col >= row)
        w2 = w2_ref[0, dirn * LANES:(dirn + 1) * LANES, :].astype(BF16)
        gb = gkb_ref[0, dirn:dirn + 1, :]
        edge = cq - 1 if dirn == 0 else 0

        def chunk(ci, carry, dirn=dirn, tri=tri, mask=mask, w2=w2, gb=gb, edge=edge):
            c = ci if dirn == 0 else nc - 1 - ci
            r0 = pl.multiple_of(c * cq, cq)
            lr = ps_ref[pl.ds(r0, cq), :].astype(BF16)
            gl = _dot(lr, w2) + gb
            g = -_softplus(-gl) * (1.0 / GLA_NORMALIZER)
            bcum = jnp.dot(tri, g, precision=HI, preferred_element_type=F32)
            blast = bcum[edge:edge + 1, :]
            qf = q_ref[pl.ds(r0, cq), :].astype(F32) * scale
            kf = k_ref[pl.ds(r0, cq), :].astype(F32)
            v = v_ref[pl.ds(r0, cq), :]
            q_in = (qf * jnp.exp(bcum)).astype(BF16)
            k_in = (kf * jnp.exp(-bcum)).astype(BF16)
            k_st = (kf * jnp.exp(blast - bcum)).astype(BF16)
            att = jnp.where(mask, _dot_nt(q_in, k_in), 0.0).astype(BF16)
            st = s_ref[...]
            o = _dot(att, v) + _dot(q_in, st.astype(BF16))
            dcol = jnp.exp(lax.dot_general(g, ones, TN_DIMS, precision=HI, preferred_element_type=F32))
            if dv > LANES:
                dfull = jnp.concatenate([dcol] * (dv // LANES), axis=1)
            else:
                dfull = dcol[:, :dv]
            s_ref[...] = st * dfull + _dot_tn(k_st, v)
            if dirn == 0:
                oacc_ref[pl.ds(r0, cq), :] = o
            else:
                ot = oacc_ref[pl.ds(r0, cq), :] + o
                ms = jnp.mean(ot * ot, axis=-1, keepdims=True)
                gf = go_ref[pl.ds(r0, cq), :].astype(F32)
                o_ref[pl.ds(r0, cq), :] = (ot * lax.rsqrt(ms + RMS_EPS) * nrm_ref[...] * (gf * _sigmoid(gf))).astype(o_ref.dtype)
            return carry

        lax.fori_loop(0, nc, chunk, 0, unroll=2)


def _gla(proj, psmall, w2pad, gkb, nrm, batch, q0, k0, v0, g0, gk_blk, dk, dv):
    t = proj.shape[0]
    seq = t // batch
    h = GLA_HEADS
    return pl.pallas_call(
        _gla_body,
        grid=(batch, h),
        in_specs=[
            pl.BlockSpec((seq, dk), lambda b, j: (b, q0 // dk + j)),
            pl.BlockSpec((seq, dk), lambda b, j: (b, k0 // dk + j)),
            pl.BlockSpec((seq, dv), lambda b, j: (b, v0 // dv + j)),
            pl.BlockSpec((seq, dv), lambda b, j: (b, g0 // dv + j)),
            pl.BlockSpec((seq, LANES), lambda b, j: (b, gk_blk)),
            pl.BlockSpec((1, 2 * LANES, dk), lambda b, j: (j, 0, 0)),
            pl.BlockSpec((1, 2, dk), lambda b, j: (j, 0, 0)),
            pl.BlockSpec((1, dv), lambda b, j: (0, 0)),
        ],
        out_specs=pl.BlockSpec((seq, dv), lambda b, j: (b, j)),
        out_shape=jax.ShapeDtypeStruct((t, h * dv), BF16),
        scratch_shapes=[pltpu.VMEM((seq, dv), F32), pltpu.VMEM((dk, dv), F32)],
        compiler_params=_params("parallel", "parallel"),
        name="gla_scan",
    )(proj, proj, proj, proj, psmall, w2pad, gkb, nrm)


def _ln_epilogue(h, xres_ref, g_ref, b_ref, alpha, outs):
    v = _layer_norm(alpha * xres_ref[...] + h, g_ref[...], b_ref[...])
    of_ref, ob_ref, opk_ref = outs
    of_ref[...] = v
    if ob_ref is not None:
        ob_ref[...] = v.astype(BF16)
    if opk_ref is not None:
        opk_ref[...] = _pack_rows(v)


def _even_out_body(y_ref, o_ref, w_ref, xres_ref, g_ref, b_ref, of_ref, ob_ref, acc_ref, *, nk1, alpha):
    k = pl.program_id(1)

    @pl.when(k == 0)
    def _():
        acc_ref[...] = jnp.zeros_like(acc_ref)

    @pl.when(k < nk1)
    def _():
        acc_ref[...] += _dot(y_ref[...], w_ref[...])

    @pl.when(k >= nk1)
    def _():
        acc_ref[...] += _dot(o_ref[...], w_ref[...])

    @pl.when(k == pl.num_programs(1) - 1)
    def _():
        _ln_epilogue(acc_ref[...], xres_ref, g_ref, b_ref, alpha, (of_ref, ob_ref, None))


def _even_out(y, o, w, xres, g, b, alpha, tm=512, tk=1024):
    t, k1 = y.shape
    k2 = o.shape[1]
    d = w.shape[1]
    tm = min(tm, t)
    tk = min(tk, k1, k2)
    assert k1 % tk == 0 and k2 % tk == 0 and t % tm == 0
    nk1, nk2 = k1 // tk, k2 // tk
    return pl.pallas_call(
        functools.partial(_even_out_body, nk1=nk1, alpha=alpha),
        grid=(t // tm, nk1 + nk2),
        in_specs=[
            pl.BlockSpec((tm, tk), lambda i, k: (i, jnp.minimum(k, nk1 - 1))),
            pl.BlockSpec((tm, tk), lambda i, k: (i, jnp.maximum(k - nk1, 0))),
            pl.BlockSpec((tk, d), lambda i, k: (k, 0)),
            pl.BlockSpec((tm, d), lambda i, k: (i, 0)),
            pl.BlockSpec((1, d), lambda i, k: (0, 0)),
            pl.BlockSpec((1, d), lambda i, k: (0, 0)),
        ],
        out_specs=[pl.BlockSpec((tm, d), lambda i, k: (i, 0)), pl.BlockSpec((tm, d), lambda i, k: (i, 0))],
        out_shape=[jax.ShapeDtypeStruct((t, d), F32), jax.ShapeDtypeStruct((t, d), BF16)],
        scratch_shapes=[pltpu.VMEM((tm, d), F32)],
        compiler_params=_params("parallel", "arbitrary"),
        name="even_out_ln",
    )(y, o, w, xres, g, b)


def _gelu_tanh(x):
    return 0.5 * x * (1.0 + jnp.tanh(0.7978845608028654 * (x + 0.044715 * x * x * x)))


def _odd_out_body(gate_ref, hf_ref, hb_ref, w_ref, xres_ref, g_ref, b_ref, of_ref, ob_ref, acc_ref, *, alpha):
    k = pl.program_id(1)

    @pl.when(k == 0)
    def _():
        acc_ref[...] = jnp.zeros_like(acc_ref)

    hsum = hf_ref[...].astype(F32) + hb_ref[...].astype(F32)
    a = (_gelu_tanh(gate_ref[...].astype(F32)) * hsum).astype(BF16)
    acc_ref[...] += _dot(a, w_ref[...])

    @pl.when(k == pl.num_programs(1) - 1)
    def _():
        _ln_epilogue(acc_ref[...], xres_ref, g_ref, b_ref, alpha, (of_ref, ob_ref, None))


def _odd_out(proj, hf, hb, w, xres, g, b, alpha, tm=512):
    t, kw = hf.shape
    d = w.shape[1]
    tm = min(tm, t)
    nk = 2 if (kw // 2) % LANES == 0 else 1
    tk = kw // nk
    return pl.pallas_call(
        functools.partial(_odd_out_body, alpha=alpha),
        grid=(t // tm, nk),
        in_specs=[
            pl.BlockSpec((tm, tk), lambda i, k: (i, k)),
            pl.BlockSpec((tm, tk), lambda i, k: (i, k)),
            pl.BlockSpec((tm, tk), lambda i, k: (i, k)),
            pl.BlockSpec((tk, d), lambda i, k: (k, 0)),
            pl.BlockSpec((tm, d), lambda i, k: (i, 0)),
            pl.BlockSpec((1, d), lambda i, k: (0, 0)),
            pl.BlockSpec((1, d), lambda i, k: (0, 0)),
        ],
        out_specs=[pl.BlockSpec((tm, d), lambda i, k: (i, 0)), pl.BlockSpec((tm, d), lambda i, k: (i, 0))],
        out_shape=[jax.ShapeDtypeStruct((t, d), F32), jax.ShapeDtypeStruct((t, d), BF16)],
        scratch_shapes=[pltpu.VMEM((tm, d), F32)],
        compiler_params=_params("parallel", "arbitrary"),
        name="odd_out_ln",
    )(proj, hf, hb, w, xres, g, b)


def _rglru_body(vf_ref, vb_ref, wa_ref, wx_ref, ba_ref, bx_ref, lam_ref, hf_ref, hb_ref, af_ref, uf_ref, ab_ref, ub_ref, hc_ref):
    tt, width = vf_ref.shape
    nb, bw = wa_ref.shape[1], wa_ref.shape[2]

    @pl.when(pl.program_id(1) == 0)
    def _():
        hc_ref[...] = jnp.zeros_like(hc_ref)

    sp = _softplus(-lam_ref[...])
    for d, (v_ref, a_s, u_s) in enumerate(((vf_ref, af_ref, uf_ref), (vb_ref, ab_ref, ub_ref))):
        for n in range(nb):
            cs = slice(n * bw, (n + 1) * bw)
            vb16 = v_ref[:, cs]
            r = _sigmoid(_dot(vb16, wa_ref[d, n]) + ba_ref[d:d + 1, cs])
            ig = _sigmoid(_dot(vb16, wx_ref[d, n]) + bx_ref[d:d + 1, cs])
            a = jnp.exp(-RG_C * r * sp[d:d + 1, cs])
            a_s[:, cs] = a
            u_s[:, cs] = jnp.sqrt(1.0 - a * a) * (ig * vb16.astype(F32))

    def step(t8, carry):
        hf, hb = carry
        for s in range(8):
            tf = t8 * 8 + s
            tb = tt - 1 - tf
            hf = af_ref[pl.ds(tf, 1), :] * hf + uf_ref[pl.ds(tf, 1), :]
            uf_ref[pl.ds(tf, 1), :] = hf
            hb = ab_ref[pl.ds(tb, 1), :] * hb + ub_ref[pl.ds(tb, 1), :]
            ub_ref[pl.ds(tb, 1), :] = hb
        return hf, hb

    hf, hb = lax.fori_loop(0, tt // 8, step, (hc_ref[0:1, :], hc_ref[1:2, :]))
    hc_ref[0:1, :] = hf
    hc_ref[1:2, :] = hb
    hf_ref[...] = uf_ref[...].astype(hf_ref.dtype)
    hb_ref[...] = ub_ref[...].astype(hb_ref.dtype)


def _rglru(vc, wa, wx, ba, bx, lam, batch, tt=256):
    t, width = vc.shape
    seq = t // batch
    tt = min(tt, seq)
    ntt = seq // tt
    nb, bw = wa.shape[1], wa.shape[2]
    full = lambda *shape: pl.BlockSpec(shape, lambda b, j: (0,) * len(shape))
    return pl.pallas_call(
        _rglru_body,
        grid=(batch, ntt),
        in_specs=[
            pl.BlockSpec((tt, width), lambda b, j: (b * ntt + j, 0)),
            pl.BlockSpec((tt, width), lambda b, j: (b * ntt + ntt - 1 - j, 0)),
            full(2, nb, bw, bw),
            full(2, nb, bw, bw),
            full(2, width),
            full(2, width),
            full(2, width),
        ],
        out_specs=[
            pl.BlockSpec((tt, width), lambda b, j: (b * ntt + j, 0)),
            pl.BlockSpec((tt, width), lambda b, j: (b * ntt + ntt - 1 - j, 0)),
        ],
        out_shape=[jax.ShapeDtypeStruct((t, width), BF16), jax.ShapeDtypeStruct((t, width), BF16)],
        scratch_shapes=[pltpu.VMEM((tt, width), F32) for _ in range(4)] + [pltpu.VMEM((8, width), F32)],
        compiler_params=_params("parallel", "arbitrary"),
        name="rglru",
    )(vc, vc, wa, wx, ba, bx, lam)


def _xattn_body(q_ref, k_ref, v_ref, wo_ref, xres_ref, g_ref, b_ref, of_ref, opk_ref, *, alpha):
    tm, d = q_ref.shape
    hd = d // XATTN_HEADS
    scale = hd ** -0.5
    acc = jnp.zeros((tm, d), F32)
    for h in range(XATTN_HEADS):
        cs = slice(h * hd, (h + 1) * hd)
        s = _dot_nt(q_ref[:, cs], k_ref[:, cs]) * scale
        s = s - jnp.max(s, axis=-1, keepdims=True)
        e = jnp.exp(s)
        pr = (e / jnp.sum(e, axis=-1, keepdims=True)).astype(BF16)
        oh = _dot(pr, v_ref[:, cs]).astype(BF16)
        acc = acc + _dot(oh, wo_ref[cs, :])
    _ln_epilogue(acc, xres_ref, g_ref, b_ref, alpha, (of_ref, None, opk_ref))


def _xattn(q, kv, wo, xres, g, b, alpha, batch, tm=512):
    t, d = q.shape
    seq = t // batch
    mem = kv.shape[0] // batch
    tm = min(tm, seq)
    return pl.pallas_call(
        functools.partial(_xattn_body, alpha=alpha),
        grid=(t // tm,),
        in_specs=[
            pl.BlockSpec((tm, d), lambda i: (i, 0)),
            pl.BlockSpec((mem, d), lambda i: (i * tm // seq, 0)),
            pl.BlockSpec((mem, d), lambda i: (i * tm // seq, 1)),
            pl.BlockSpec((d, d), lambda i: (0, 0)),
            pl.BlockSpec((tm, d), lambda i: (i, 0)),
            pl.BlockSpec((1, d), lambda i: (0, 0)),
            pl.BlockSpec((1, d), lambda i: (0, 0)),
        ],
        out_specs=[pl.BlockSpec((tm, d), lambda i: (i, 0)), pl.BlockSpec((tm, d // 2), lambda i: (i, 0))],
        out_shape=[jax.ShapeDtypeStruct((t, d), F32), jax.ShapeDtypeStruct((t, d // 2), I32)],
        compiler_params=_params("parallel"),
        name="xattn_out_ln",
    )(q, kv, kv, wo, xres, g, b)


def _router_body(x_ref, w_ref, b_ref, ri_ref, rg_ref, cnt_ref, carry_ref):
    tr = x_ref.shape[0]

    @pl.when(pl.program_id(0) == 0)
    def _():
        carry_ref[...] = jnp.zeros_like(carry_ref)

    logits = jnp.dot(x_ref[...], w_ref[...], precision=HI, preferred_element_type=F32) + b_ref[...]
    lane = lax.broadcasted_iota(I32, (tr, LANES), 1).astype(F32)
    work = logits
    sels, vals, idxs = [], [], []
    for _ in range(TOP_K):
        m = jnp.max(work, axis=-1, keepdims=True)
        idx = jnp.min(jnp.where(work == m, lane, float(LANES)), axis=-1, keepdims=True)
        sel = lane == idx
        sels.append(sel)
        vals.append(m)
        idxs.append(idx)
        work = jnp.where(sel, NEG_BIG, work)
    exps = [jnp.exp(v - vals[0]) for v in vals]
    den = exps[0]
    for e in exps[1:]:
        den = den + e
    onehot = jnp.zeros((tr, LANES), F32)
    for sel in sels:
        onehot = onehot + sel.astype(F32)
    row = lax.broadcasted_iota(I32, (tr, tr), 0)
    col = lax.broadcasted_iota(I32, (tr, tr), 1)
    strict = (col < row).astype(BF16)
    before = _dot(strict, onehot.astype(BF16)) + carry_ref[...]
    ri = jnp.zeros((tr, LANES), F32)
    rg = jnp.zeros((tr, LANES), F32)
    for k in range(TOP_K):
        rank = jnp.sum(jnp.where(sels[k], before, 0.0), axis=-1, keepdims=True)
        ri = jnp.where(lane == float(k), idxs[k], ri)
        ri = jnp.where(lane == float(TOP_K + k), rank, ri)
        rg = jnp.where(lane == float(k), exps[k] / den, rg)
    ri_ref[...] = ri.astype(I32)
    rg_ref[...] = rg
    total = carry_ref[...] + jnp.sum(onehot, axis=0, keepdims=True)
    carry_ref[...] = total
    cnt_ref[...] = total


def _router(x, w_pad, b_pad, tr=512):
    t, d = x.shape
    tr = min(tr, t)
    return pl.pallas_call(
        _router_body,
        grid=(t // tr,),
        in_specs=[
            pl.BlockSpec((tr, d), lambda i: (i, 0)),
            pl.BlockSpec((d, LANES), lambda i: (0, 0)),
            pl.BlockSpec((1, LANES), lambda i: (0, 0)),
        ],
        out_specs=[
            pl.BlockSpec((tr, LANES), lambda i: (i, 0)),
            pl.BlockSpec((tr, LANES), lambda i: (i, 0)),
            pl.BlockSpec((1, LANES), lambda i: (0, 0)),
        ],
        out_shape=[
            jax.ShapeDtypeStruct((t, LANES), I32),
            jax.ShapeDtypeStruct((t, LANES), F32),
            jax.ShapeDtypeStruct((1, LANES), F32),
        ],
        scratch_shapes=[pltpu.VMEM((1, LANES), F32)],
        compiler_params=_params("arbitrary"),
        name="moe_router",
    )(x, w_pad, b_pad)


def _dispatch_body(pos_ref, xpk_ref, xs_in_ref, xs_ref, sem):
    del xs_in_ref
    ts = xpk_ref.shape[0]

    def issue(t, carry):
        for k in range(TOP_K):
            pltpu.make_async_copy(xpk_ref.at[pl.ds(t, 1)], xs_ref.at[pl.ds(pos_ref[0, 0, TOP_K * t + k], 1)], sem).start()
        return carry

    lax.fori_loop(0, ts, issue, 0, unroll=2)
    for k in range(TOP_K):
        pltpu.make_async_copy(xpk_ref, xs_ref.at[pl.ds(0, ts)], sem).wait()


def _dispatch(xpk, pos, n_rows, ts=256):
    t, dh = xpk.shape
    ts = min(ts, t)
    pos3 = pos.reshape(t // ts, 1, ts * TOP_K)
    xs0 = jnp.zeros((n_rows, dh), I32)
    return pl.pallas_call(
        _dispatch_body,
        grid=(t // ts,),
        in_specs=[
            pl.BlockSpec((1, 1, ts * TOP_K), lambda i: (i, 0, 0), memory_space=pltpu.SMEM),
            pl.BlockSpec((ts, dh), lambda i: (i, 0)),
            pl.BlockSpec(memory_space=pl.ANY),
        ],
        out_specs=pl.BlockSpec(memory_space=pl.ANY),
        out_shape=jax.ShapeDtypeStruct((n_rows, dh), I32),
        scratch_shapes=[pltpu.SemaphoreType.DMA(())],
        input_output_aliases={2: 0},
        compiler_params=_params("arbitrary"),
        name="moe_dispatch",
    )(pos3, xpk, xs0)


def _moe_body(te_ref, tr_ref, nu_ref, xs_ref, wg_ref, wl_ref, bg_ref, bl_ref, wd_ref, bd_ref, o_ref, acc_ref, *, sub):
    del te_ref
    i = pl.program_id(0)
    f = pl.program_id(1)
    tm, dh = xs_ref.shape

    @pl.when(i < nu_ref[0])
    def _():
        @pl.when(f == 0)
        def _():
            acc_ref[...] = jnp.broadcast_to(bd_ref[0, 0], acc_ref.shape)

        nsub = (tr_ref[i] + sub - 1) // sub

        def body(s, carry):
            r0 = pl.multiple_of(s * sub, sub)
            lo, hi = _unpack_rows(xs_ref[pl.ds(r0, sub), :])
            lo = lo.astype(BF16)
            hi = hi.astype(BF16)
            hg = (_dot(lo, wg_ref[0, 0, 0:dh, :].astype(BF16)) + _dot(hi, wg_ref[0, 0, dh:2 * dh, :].astype(BF16))
                  + bg_ref[0, 0])
            hl = (_dot(lo, wl_ref[0, 0, 0:dh, :].astype(BF16)) + _dot(hi, wl_ref[0, 0, dh:2 * dh, :].astype(BF16))
                  + bl_ref[0, 0])
            glu = jnp.minimum(hg, SWIGLU_LIMIT)
            lin = jnp.clip(hl, -SWIGLU_LIMIT, SWIGLU_LIMIT)
            act = ((lin + 1.0) * glu * _sigmoid(SWIGLU_ALPHA * glu)).astype(BF16)
            acc_ref[pl.ds(r0, sub), :] += _dot(act, wd_ref[0, 0].astype(BF16))
            return carry

        lax.fori_loop(0, nsub, body, 0)

        @pl.when(f == pl.num_programs(1) - 1)
        def _():
            for s in range(tm // sub):
                o_ref[s * sub:(s + 1) * sub, :] = _pack_rows(acc_ref[s * sub:(s + 1) * sub, :])


def _moe_experts(xs, w_gu, b_gu, w_dn, b_dn, layer, tile_e, tile_rows, n_used, tm, tf, sub):
    n_rows, dh = xs.shape
    d = 2 * dh
    depth, e, _, ff2 = w_gu.shape
    ff = ff2 // 2
    tf = min(tf, ff)
    nf = ff // tf
    nt = n_rows // tm

    def tile_idx(i, f, te, tr, nu):
        return (jnp.minimum(i, nu[0] - 1), 0)

    def fidx(i, f, nu):
        return jnp.where(i < nu[0], f, nf - 1)

    grid_spec = pltpu.PrefetchScalarGridSpec(
        num_scalar_prefetch=3,
        grid=(nt, nf),
        in_specs=[
            pl.BlockSpec((tm, dh), tile_idx),
            pl.BlockSpec((1, 1, d, tf), lambda i, f, te, tr, nu: (layer, te[i], 0, fidx(i, f, nu))),
            pl.BlockSpec((1, 1, d, tf), lambda i, f, te, tr, nu: (layer, te[i], 0, nf + fidx(i, f, nu))),
            pl.BlockSpec((1, 1, 1, tf), lambda i, f, te, tr, nu: (layer, te[i], 0, fidx(i, f, nu))),
            pl.BlockSpec((1, 1, 1, tf), lambda i, f, te, tr, nu: (layer, te[i], 0, nf + fidx(i, f, nu))),
            pl.BlockSpec((1, 1, tf, d), lambda i, f, te, tr, nu: (layer, te[i], fidx(i, f, nu), 0)),
            pl.BlockSpec((1, 1, 1, d), lambda i, f, te, tr, nu: (layer, te[i], 0, 0)),
        ],
        out_specs=pl.BlockSpec((tm, dh), tile_idx),
        scratch_shapes=[pltpu.VMEM((tm, d), F32)],
    )
    return pl.pallas_call(
        functools.partial(_moe_body, sub=sub),
        grid_spec=grid_spec,
        out_shape=jax.ShapeDtypeStruct((n_rows, dh), I32),
        input_output_aliases={3: 0},
        compiler_params=_params("arbitrary", "arbitrary"),
        name="moe_experts",
    )(tile_e, tile_rows, n_used, xs, w_gu, w_gu, b_gu.reshape(depth, e, 1, ff2), b_gu.reshape(depth, e, 1, ff2), w_dn,
      b_dn.reshape(depth, e, 1, d))


def _combine_body(pos_ref, pos_next_ref, rg_ref, outs_ref, xres_ref, g_ref, b_ref, of_ref, ob_ref, buf_ref, sem, *,
                  alpha):
    i = pl.program_id(0)
    tc = rg_ref.shape[0]
    slot = i % 2

    def gather_rows(p_ref, dst_slot):
        def issue(t, carry):
            for k in range(TOP_K):
                pltpu.make_async_copy(outs_ref.at[pl.ds(p_ref[0, 0, TOP_K * t + k], 1)],
                                      buf_ref.at[dst_slot, k, pl.ds(t, 1)], sem.at[dst_slot]).start()
            return carry

        lax.fori_loop(0, tc, issue, 0, unroll=2)

    @pl.when(i == 0)
    def _():
        gather_rows(pos_ref, 0)

    @pl.when(i + 1 < pl.num_programs(0))
    def _():
        gather_rows(pos_next_ref, 1 - slot)

    for k in range(TOP_K):
        pltpu.make_async_copy(outs_ref.at[pl.ds(0, tc)], buf_ref.at[slot, k], sem.at[slot]).wait()
    gates = rg_ref[...]
    ylo = yhi = None
    for k in range(TOP_K):
        lo, hi = _unpack_rows(buf_ref[slot, k])
        gk = gates[:, k:k + 1]
        ylo = gk * lo if ylo is None else ylo + gk * lo
        yhi = gk * hi if yhi is None else yhi + gk * hi
    y = jnp.concatenate([ylo, yhi], axis=1)
    _ln_epilogue(y, xres_ref, g_ref, b_ref, alpha, (of_ref, ob_ref, None))


def _combine(outs, pos, rg, xres, g, b, alpha, tc=128):
    t, d = xres.shape
    tc = min(tc, t)
    dh = d // 2
    nsteps = t // tc
    pos3 = pos.reshape(nsteps, 1, tc * TOP_K)
    return pl.pallas_call(
        functools.partial(_combine_body, alpha=alpha),
        grid=(nsteps,),
        in_specs=[
            pl.BlockSpec((1, 1, tc * TOP_K), lambda i: (i, 0, 0), memory_space=pltpu.SMEM),
            pl.BlockSpec((1, 1, tc * TOP_K), lambda i: (jnp.minimum(i + 1, nsteps - 1), 0, 0), memory_space=pltpu.SMEM),
            pl.BlockSpec((tc, LANES), lambda i: (i, 0)),
            pl.BlockSpec(memory_space=pl.ANY),
            pl.BlockSpec((tc, d), lambda i: (i, 0)),
            pl.BlockSpec((1, d), lambda i: (0, 0)),
            pl.BlockSpec((1, d), lambda i: (0, 0)),
        ],
        out_specs=[pl.BlockSpec((tc, d), lambda i: (i, 0)), pl.BlockSpec((tc, d), lambda i: (i, 0))],
        out_shape=[jax.ShapeDtypeStruct((t, d), F32), jax.ShapeDtypeStruct((t, d), BF16)],
        scratch_shapes=[pltpu.VMEM((2, TOP_K, tc, dh), I32), pltpu.SemaphoreType.DMA((2,))],
        compiler_params=_params("arbitrary"),
        name="moe_combine_ln",
    )(pos3, pos3, rg, outs, xres, g, b)


def _moe_layer(xf, xpk, router_w, router_b, w_gu, b_gu, w_dn, b_dn, layer, g, b, alpha, tm, tf, sub):
    t, d = xf.shape
    e = router_w.shape[1]
    w_pad = jnp.zeros((d, LANES), F32).at[:, :e].set(router_w)
    b_pad = jnp.full((1, LANES), -1e30, F32).at[0, :e].set(router_b)
    ri, rg, cnt = _router(xf, w_pad, b_pad)
    counts = cnt[0, :e].astype(I32)
    nt_e = (counts + tm - 1) // tm
    per_e = ((counts + jnp.maximum(nt_e, 1) * sub - 1) // (jnp.maximum(nt_e, 1) * sub)) * sub
    per_e = jnp.maximum(per_e, sub)
    tile_end = jnp.cumsum(nt_e)
    tile_start = tile_end - nt_e
    n_used = tile_end[-1]
    nt = (t * TOP_K) // tm + e
    e_idx = ri[:, 0:TOP_K]
    rank = ri[:, TOP_K:2 * TOP_K]
    per_a = per_e[e_idx]
    pos = (tile_start[e_idx] + rank // per_a) * tm + rank % per_a
    tid = jnp.minimum(jnp.arange(nt, dtype=I32), n_used - 1)
    tile_e = jnp.minimum(jnp.searchsorted(tile_end, tid, side="right"), e - 1).astype(I32)
    tile_rows = jnp.clip(counts[tile_e] - (tid - tile_start[tile_e]) * per_e[tile_e], 0, per_e[tile_e]).astype(I32)
    xs = _dispatch(xpk, pos, nt * tm)
    outs = _moe_experts(xs, w_gu, b_gu, w_dn, b_dn, layer, tile_e, tile_rows, n_used.reshape(1).astype(I32), tm, tf,
                        sub)
    return _combine(outs, pos, rg, xf, g, b, alpha)


def _xattn_layer(xf, xb, mem_b, wq, wkv, wo, g, b, alpha, batch):
    q = _matmul(xb, wq.astype(BF16), BF16)
    kv = _matmul(mem_b, wkv.astype(BF16), BF16)
    return _xattn(q, kv, wo.astype(BF16), xf, g, b, alpha, batch)


def _even_mixer(xf, xb, w_in, conv_w, conv_b, a_log, dt_bias, d_skip, ssd_norm, gk_w2, gk_b, gla_norm, w_out, g, b,
                alpha, batch):
    d = xf.shape[1]
    grp = SSD_GROUPS
    heads = a_log.shape[1]
    r_heads = heads // grp
    d_ssd = heads * SSD_HEAD_DIM
    n_state = (conv_w.shape[1] - d_ssd) // (2 * grp)
    conv_ch = d_ssd + 2 * grp * n_state
    rank = gk_w2.shape[1]
    dk = gk_w2.shape[2] // GLA_HEADS
    dv = gla_norm.shape[0]
    o_xbc = d_ssd
    o_dt = o_xbc + conv_ch
    o_q = o_dt + 2 * heads
    o_k = o_q + GLA_HEADS * dk
    o_v = o_k + GLA_HEADS * dk
    o_g = o_v + GLA_HEADS * dv
    o_lr = o_g + GLA_HEADS * dv
    assert 2 * r_heads <= LANES and 2 * rank <= LANES and o_lr + 2 * rank == w_in.shape[1]
    w_big = jnp.concatenate([w_in[:, :o_dt], w_in[:, o_q:o_lr]], axis=1).astype(BF16)
    w_dt = w_in[:, o_dt:o_q].reshape(d, 2, grp, r_heads).transpose(0, 2, 1, 3).reshape(d, grp, 2 * r_heads)
    w_dt = jnp.pad(w_dt, ((0, 0), (0, 0), (0, LANES - 2 * r_heads))).reshape(d, grp * LANES)
    w_lr = jnp.pad(w_in[:, o_lr:], ((0, 0), (0, LANES - 2 * rank)))
    w_small = jnp.concatenate([w_dt, w_lr], axis=1).astype(BF16)
    proj = _matmul(xb, w_big, BF16)
    psmall = _matmul(xb, w_small, F32)

    xbc = _dwconv(proj, o_xbc, conv_ch, conv_w, conv_b, batch, act=True)

    def group_lanes(p):
        p = p.reshape(2, grp, r_heads).transpose(1, 0, 2).reshape(grp, 1, 2 * r_heads)
        return jnp.pad(p, ((0, 0), (0, 0), (0, LANES - 2 * r_heads)))

    y = _ssd(xbc, proj, psmall, group_lanes(a_log), group_lanes(dt_bias),
             jnp.repeat(d_skip, SSD_HEAD_DIM).reshape(1, d_ssd), ssd_norm.reshape(1, d_ssd), batch, d_ssd, n_state)

    w2 = gk_w2.reshape(2, rank, GLA_HEADS, dk).transpose(2, 0, 1, 3)
    w2pad = jnp.zeros((GLA_HEADS, 2, LANES, dk), F32)
    for dirn in range(2):
        w2pad = w2pad.at[:, dirn, dirn * rank:(dirn + 1) * rank, :].set(w2[:, dirn])
    w2pad = w2pad.reshape(GLA_HEADS, 2 * LANES, dk)
    gkb = gk_b.reshape(2, GLA_HEADS, dk).transpose(1, 0, 2)
    q0 = o_dt
    k0 = q0 + GLA_HEADS * dk
    v0 = k0 + GLA_HEADS * dk
    g0 = v0 + GLA_HEADS * dv
    o = _gla(proj, psmall, w2pad, gkb, gla_norm.reshape(1, dv), batch, q0, k0, v0, g0, grp, dk, dv)
    return _even_out(y, o, w_out.astype(BF16), xf, g, b, alpha)


def _odd_mixer(xf, xb, w_in, conv_w, conv_b, wa, ba, wx, bx, lam, w_out, g, b, alpha, batch):
    width = conv_w.shape[1]
    proj = _matmul(xb, w_in.astype(BF16), BF16)
    vc = _dwconv(proj, width, width, conv_w, conv_b, batch, act=False)
    hf, hb = _rglru(vc, wa.astype(BF16), wx.astype(BF16), ba, bx, lam, batch)
    return _odd_out(proj, hf, hb, w_out.astype(BF16), xf, g, b, alpha)


def kernel(x, mem, ln_g, ln_b, even_w_in, even_conv_w, even_conv_b, ssd_a_log, ssd_dt_bias, ssd_d, ssd_norm, gla_gk_w2, gla_gk_b, gla_norm, even_w_out, odd_w_in, odd_conv_w, odd_conv_b, rg_wa, rg_ba, rg_wx, rg_bx, rg_lam, odd_w_out, xattn_wq, xattn_wkv, xattn_wo, router_w, router_b, moe_w_gu, moe_b_gu, moe_w_dn, moe_b_dn):
    batch, seq, d = x.shape
    depth = ln_g.shape[0]
    alpha = float((2 * depth) ** 0.25)
    xf = x.reshape(batch * seq, d)
    xb = xf.astype(BF16)
    mem_b = mem.reshape(-1, d).astype(BF16)
    for layer in range(depth):
        i = layer // 2
        lg = lambda s: ln_g[layer, s].reshape(1, d)
        lb = lambda s: ln_b[layer, s].reshape(1, d)
        if layer % 2 == 0:
            xf, xb = _even_mixer(xf, xb, even_w_in[i], even_conv_w[i], even_conv_b[i], ssd_a_log[i], ssd_dt_bias[i],
                                 ssd_d[i], ssd_norm[i], gla_gk_w2[i], gla_gk_b[i], gla_norm[i], even_w_out[i],
                                 lg(0), lb(0), alpha, batch)
        else:
            xf, xb = _odd_mixer(xf, xb, odd_w_in[i], odd_conv_w[i], odd_conv_b[i], rg_wa[i], rg_ba[i], rg_wx[i],
                                rg_bx[i], rg_lam[i], odd_w_out[i], lg(0), lb(0), alpha, batch)
        xf, xpk = _xattn_layer(xf, xb, mem_b, xattn_wq[layer], xattn_wkv[layer], xattn_wo[layer], lg(1), lb(1), alpha,
                               batch)
        xf, xb = _moe_layer(xf, xpk, router_w[layer], router_b[layer], moe_w_gu, moe_b_gu, moe_w_dn, moe_b_dn, layer,
                            lg(2), lb(2), alpha, MOE_TILE, MOE_FF_TILE, MOE_SUB)
    return xf.reshape(batch, seq, d)
```

```python
import functools

import jax
import jax.numpy as jnp
from jax import lax
from jax.experimental import pallas as pl
from jax.experimental.pallas import tpu as pltpu

F32 = jnp.float32
BF16 = jnp.bfloat16
I32 = jnp.int32

SSD_HEAD_DIM = 64
SSD_GROUPS = 4
SSD_CHUNK = 128
GLA_HEADS = 4
GLA_NORMALIZER = 16.0
GLA_CHUNK = 64
GLA_GROUP = 4
RG_C = 8.0
XATTN_HEADS = 4
TOP_K = 4
SWIGLU_ALPHA = 1.702
SWIGLU_LIMIT = 7.0
LN_EPS = 1e-5
RMS_EPS = 1e-6

LANES = 128
VMEM_LIMIT_BYTES = 56 * 1024 * 1024
CONV_HALO = 16
MOE_TILE = 1536
MOE_FF_TILE = 256
MOE_SUB = 256
NEG_BIG = -3.0e38
HI = lax.Precision.HIGHEST

TN_DIMS = (((0,), (0,)), ((), ()))
NT_DIMS = (((1,), (1,)), ((), ()))


def _params(*sem):
    return pltpu.CompilerParams(dimension_semantics=sem, vmem_limit_bytes=VMEM_LIMIT_BYTES)


def _dot(a, b):
    return jnp.dot(a, b, preferred_element_type=F32)


def _dot_nt(a, b):
    return lax.dot_general(a, b, NT_DIMS, preferred_element_type=F32)


def _dot_tn(a, b):
    return lax.dot_general(a, b, TN_DIMS, preferred_element_type=F32)


def _sigmoid(x):
    return 1.0 / (1.0 + jnp.exp(-x))


def _softplus(x):
    return jnp.maximum(x, 0.0) + jnp.log(1.0 + jnp.exp(-jnp.abs(x)))


def _split3(x):
    hi = x.astype(BF16)
    rem = x - hi.astype(F32)
    mid = rem.astype(BF16)
    lo = (rem - mid.astype(F32)).astype(BF16)
    return hi, mid, lo


def _tri_cumsum(tri, x):
    hi, mid, lo = _split3(x)
    return _dot(tri, hi) + _dot(tri, mid) + _dot(tri, lo)


def _tri_cumsum_tn(x, tri_t):
    hi, mid, lo = _split3(x)
    return _dot_tn(hi, tri_t) + _dot_tn(mid, tri_t) + _dot_tn(lo, tri_t)


def _layer_norm(v, g, b):
    mu = jnp.mean(v, axis=-1, keepdims=True)
    c = v - mu
    var = jnp.mean(c * c, axis=-1, keepdims=True)
    return c * lax.rsqrt(var + LN_EPS) * g + b


def _pack_rows(v):
    h = v.shape[1] // 2
    lo = pltpu.bitcast(v[:, :h].astype(BF16).astype(F32), I32)
    hi = pltpu.bitcast(v[:, h:].astype(BF16).astype(F32), I32)
    return (lo & jnp.int32(-65536)) | (lax.shift_right_logical(hi, jnp.int32(16)))


def _unpack_rows(w):
    lo = pltpu.bitcast(w & jnp.int32(-65536), F32)
    hi = pltpu.bitcast(lax.shift_left(w, jnp.int32(16)), F32)
    return lo, hi


def _largest_tile(n, cap, *also):
    for c in range(min(cap, n) // LANES * LANES, 0, -LANES):
        if n % c == 0 and all(a % c == 0 for a in also):
            return c
    return n


def _mm_body(a_ref, w_ref, o_ref):
    o_ref[...] = _dot(a_ref[...], w_ref[...]).astype(o_ref.dtype)


def _matmul(a, w, out_dtype, tm=1024, tn=1024):
    m, k = a.shape
    n = w.shape[1]
    tm = min(tm, m)
    tn = _largest_tile(n, tn)
    assert m % tm == 0
    return pl.pallas_call(
        _mm_body,
        grid=(n // tn, m // tm),
        in_specs=[pl.BlockSpec((tm, k), lambda j, i: (i, 0)), pl.BlockSpec((k, tn), lambda j, i: (0, j))],
        out_specs=pl.BlockSpec((tm, tn), lambda j, i: (i, j)),
        out_shape=jax.ShapeDtypeStruct((m, n), out_dtype),
        compiler_params=_params("parallel", "parallel"),
        name="proj_matmul",
    )(a, w)


def _conv_body(x_ref, w_ref, b_ref, o_ref, *, act, rows):
    seq = x_ref.shape[0]
    nchunks = seq // rows
    tot = rows + 2 * CONV_HALO
    w = w_ref[...]
    b = b_ref[...]

    def body(i, carry):
        r0 = pl.multiple_of(i * rows, rows)
        main = x_ref[pl.ds(r0, rows), :].astype(F32)
        p0 = pl.multiple_of(jnp.maximum(r0 - CONV_HALO, 0), CONV_HALO)
        prev = x_ref[pl.ds(p0, CONV_HALO), :].astype(F32)
        prev = jnp.where(i > 0, prev, 0.0)
        n0 = pl.multiple_of(jnp.minimum(r0 + rows, seq - CONV_HALO), CONV_HALO)
        nxt = x_ref[pl.ds(n0, CONV_HALO), :].astype(F32)
        nxt = jnp.where(i < nchunks - 1, nxt, 0.0)
        ext = jnp.concatenate([prev, main, nxt], axis=0)
        acc = b + w[1:2] * main
        for j in (0, 2, 3):
            shift = (1 - j) % tot
            acc = acc + w[j:j + 1] * pltpu.roll(ext, shift, 0)[CONV_HALO:CONV_HALO + rows]
        if act:
            acc = acc * _sigmoid(acc)
        o_ref[pl.ds(r0, rows), :] = acc.astype(o_ref.dtype)
        return carry

    lax.fori_loop(0, nchunks, body, 0)


def _dwconv(x, col0, width, conv_w, conv_b, batch, act, cw=512):
    t = x.shape[0]
    seq = t // batch
    cw = _largest_tile(width, cw, col0)
    assert width % cw == 0 and col0 % cw == 0
    rows = min(256, seq)
    cb0 = col0 // cw
    return pl.pallas_call(
        functools.partial(_conv_body, act=act, rows=rows),
        grid=(batch, width // cw),
        in_specs=[
            pl.BlockSpec((seq, cw), lambda b, j: (b, cb0 + j)),
            pl.BlockSpec((conv_w.shape[0], cw), lambda b, j: (0, j)),
            pl.BlockSpec((1, cw), lambda b, j: (0, j)),
        ],
        out_specs=pl.BlockSpec((seq, cw), lambda b, j: (b, j)),
        out_shape=jax.ShapeDtypeStruct((t, width), BF16),
        compiler_params=_params("parallel", "parallel"),
        name="dwconv",
    )(x, conv_w, conv_b.reshape(1, -1))


def _ssd_body(xs_ref, b_ref, c_ref, z_ref, ps_ref, alog_ref, dtb_ref, dsk_ref, nrm_ref, y_ref, yacc_ref, h_ref):
    seq, hp = xs_ref.shape
    q = SSD_CHUNK
    nc = seq // q
    p = SSD_HEAD_DIM
    r_heads = hp // p
    a_neg = -jnp.exp(alog_ref[0])
    dtb = dtb_ref[0]
    row = lax.broadcasted_iota(I32, (q, q), 0)
    col = lax.broadcasted_iota(I32, (q, q), 1)
    lower = (col <= row).astype(BF16)
    upper = (col >= row).astype(BF16)
    e_row = lax.broadcasted_iota(I32, (LANES, hp), 0)
    e_col = lax.broadcasted_iota(I32, (LANES, hp), 1) // p
    pw = min(LANES, hp)
    piece_head = lax.broadcasted_iota(I32, (q, pw), 1) // p

    h_ref[...] = jnp.zeros_like(h_ref)

    def chunk_dir(c, dirn, final):
        tri, tri_t = (lower, upper) if dirn == 0 else (upper, lower)
        mask = (col <= row) if dirn == 0 else (col >= row)
        expand = (e_row == e_col + dirn * r_heads).astype(BF16)
        edge = q - 1 if dirn == 0 else 0
        r0 = pl.multiple_of(c * q, q)
        x = xs_ref[pl.ds(r0, q), :]
        xf = x.astype(F32)
        bm = b_ref[pl.ds(r0, q), :]
        cm = c_ref[pl.ds(r0, q), :]
        dt = _softplus(ps_ref[pl.ds(r0, q), :] + dtb)
        dta = dt * a_neg
        acum = _tri_cumsum(tri, dta)
        acum_t = _tri_cumsum_tn(dta, tri_t)
        alast = acum[edge:edge + 1, :]
        stack = jnp.concatenate([dt, jnp.exp(acum), dt * jnp.exp(alast - acum)], axis=0).astype(BF16)
        ex = _dot(stack, expand)
        xdt = (xf * ex[0:q]).astype(BF16)
        eac = ex[q:2 * q]
        xw = (xf * ex[2 * q:3 * q]).astype(BF16)
        cb = _dot_nt(cm, bm)
        yd = []
        for j in range(hp // pw):
            xpiece = xdt[:, j * pw:(j + 1) * pw]
            ypiece = None
            for a in range(pw // p):
                cr = dirn * r_heads + j * (pw // p) + a
                seg = acum[:, cr:cr + 1] - acum_t[cr:cr + 1, :]
                dec = jnp.where(mask, jnp.exp(jnp.minimum(seg, 0.0)), 0.0)
                wgt = (cb * dec).astype(BF16)
                xa = xpiece if pw == p else jnp.where(piece_head == a, xpiece, jnp.zeros_like(xpiece))
                ya = _dot(wgt, xa)
                ypiece = ya if ypiece is None else ypiece + ya
            yd.append(ypiece)
        y = jnp.concatenate(yd, axis=1) if len(yd) > 1 else yd[0]
        hst = h_ref[dirn]
        y = y + _dot(cm, hst.astype(BF16)) * eac
        ea = jnp.broadcast_to(jnp.exp(alast), (8, LANES))
        ea_hi = ea.astype(BF16)
        ea_lo = (ea - ea_hi.astype(F32)).astype(BF16)
        hdec = (_dot(ea_hi, expand) + _dot(ea_lo, expand))[0:1]
        h_ref[dirn] = hst * hdec + _dot_tn(bm, xw)
        if not final:
            yacc_ref[pl.ds(r0, q), :] = y
        else:
            yt = yacc_ref[pl.ds(r0, q), :] + y + dsk_ref[...] * xf
            zf = z_ref[pl.ds(r0, q), :].astype(F32)
            yt = yt * (zf * _sigmoid(zf))
            ms = jnp.mean(yt * yt, axis=-1, keepdims=True)
            y_ref[pl.ds(r0, q), :] = (yt * lax.rsqrt(ms + RMS_EPS) * nrm_ref[...]).astype(y_ref.dtype)

    def both_dirs(final):
        def body(i, carry):
            chunk_dir(i, 0, final)
            chunk_dir(nc - 1 - i, 1, final)
            return carry
        return body

    assert nc % 2 == 0
    lax.fori_loop(0, nc // 2, both_dirs(False), 0)
    lax.fori_loop(nc // 2, nc, both_dirs(True), 0)


def _ssd(xbc, proj, psmall, alog, dtb, dskip, nrm, batch, d_ssd, n_state):
    t = xbc.shape[0]
    seq = t // batch
    g = SSD_GROUPS
    hp = d_ssd // g
    nb0 = d_ssd // n_state
    return pl.pallas_call(
        _ssd_body,
        grid=(batch, g),
        in_specs=[
            pl.BlockSpec((seq, hp), lambda b, j: (b, j)),
            pl.BlockSpec((seq, n_state), lambda b, j: (b, nb0 + j)),
            pl.BlockSpec((seq, n_state), lambda b, j: (b, nb0 + g + j)),
            pl.BlockSpec((seq, hp), lambda b, j: (b, j)),
            pl.BlockSpec((seq, LANES), lambda b, j: (b, j)),
            pl.BlockSpec((1, 1, LANES), lambda b, j: (j, 0, 0)),
            pl.BlockSpec((1, 1, LANES), lambda b, j: (j, 0, 0)),
            pl.BlockSpec((1, hp), lambda b, j: (0, j)),
            pl.BlockSpec((1, hp), lambda b, j: (0, j)),
        ],
        out_specs=pl.BlockSpec((seq, hp), lambda b, j: (b, j)),
        out_shape=jax.ShapeDtypeStruct((t, d_ssd), BF16),
        scratch_shapes=[pltpu.VMEM((seq, hp), F32), pltpu.VMEM((2, n_state, hp), F32)],
        compiler_params=_params("parallel", "parallel"),
        name="ssd_scan",
    )(xbc, xbc, xbc, proj, psmall, alog, dtb, dskip, nrm)


def _gla_body(q_ref, k_ref, v_ref, go_ref, ps_ref, w2_ref, gkb_ref, nrm_ref, o_ref, oacc_ref, s_ref):
    seq, dk = q_ref.shape
    dv = v_ref.shape[1]
    cq = GLA_CHUNK
    gq = GLA_GROUP * cq
    ng = seq // gq
    scale = dk ** -0.5
    row = lax.broadcasted_iota(I32, (gq, gq), 0)
    col = lax.broadcasted_iota(I32, (gq, gq), 1)
    same_chunk = (row // cq) == (col // cq)
    masks = (same_chunk & (col <= row), same_chunk & (col >= row))
    tris = tuple(m.astype(BF16) for m in masks)

    s_ref[...] = jnp.zeros_like(s_ref)

    def group_dir(gi, dirn, final):
        w2 = w2_ref[0, dirn * LANES:(dirn + 1) * LANES, :].astype(BF16)
        gb = gkb_ref[0, dirn:dirn + 1, :]
        edge = cq - 1 if dirn == 0 else 0
        r0 = pl.multiple_of(gi * gq, gq)
        lr = ps_ref[pl.ds(r0, gq), :].astype(BF16)
        gl = _dot(lr, w2) + gb
        g = -_softplus(-gl) * (1.0 / GLA_NORMALIZER)
        bcum = _tri_cumsum(tris[dirn], g)
        qf = q_ref[pl.ds(r0, gq), :].astype(F32) * scale
        kf = k_ref[pl.ds(r0, gq), :].astype(F32)
        v = v_ref[pl.ds(r0, gq), :]
        q_in = (qf * jnp.exp(bcum)).astype(BF16)
        k_in = (kf * jnp.exp(-bcum)).astype(BF16)
        att = jnp.where(masks[dirn], _dot_nt(q_in, k_in), 0.0).astype(BF16)
        o_intra = _dot(att, v)
        order = range(GLA_GROUP) if dirn == 0 else range(GLA_GROUP - 1, -1, -1)
        for j in order:
            rows = slice(j * cq, (j + 1) * cq)
            blast = bcum[j * cq + edge:j * cq + edge + 1, :]
            k_st = (kf[rows] * jnp.exp(blast - bcum[rows])).astype(BF16)
            st = s_ref[dirn]
            o = o_intra[rows] + _dot_nt(q_in[rows], st.astype(BF16))
            s_ref[dirn] = st * jnp.exp(blast) + _dot_tn(v[rows], k_st)
            rj = pl.multiple_of(r0 + j * cq, cq)
            if not final:
                oacc_ref[pl.ds(rj, cq), :] = o
            else:
                ot = oacc_ref[pl.ds(rj, cq), :] + o
                ms = jnp.mean(ot * ot, axis=-1, keepdims=True)
                gf = go_ref[pl.ds(rj, cq), :].astype(F32)
                o_ref[pl.ds(rj, cq), :] = (ot * lax.rsqrt(ms + RMS_EPS) * nrm_ref[...] * (gf * _sigmoid(gf))).astype(o_ref.dtype)

    def both_dirs(final):
        def body(i, carry):
            group_dir(i, 0, final)
            group_dir(ng - 1 - i, 1, final)
            return carry
        return body

    assert ng % 2 == 0
    lax.fori_loop(0, ng // 2, both_dirs(False), 0)
    lax.fori_loop(ng // 2, ng, both_dirs(True), 0)


def _gla(proj, psmall, w2pad, gkb, nrm, batch, q0, k0, v0, g0, gk_blk, dk, dv):
    t = proj.shape[0]
    seq = t // batch
    h = GLA_HEADS
    return pl.pallas_call(
        _gla_body,
        grid=(batch, h),
        in_specs=[
            pl.BlockSpec((seq, dk), lambda b, j: (b, q0 // dk + j)),
            pl.BlockSpec((seq, dk), lambda b, j: (b, k0 // dk + j)),
            pl.BlockSpec((seq, dv), lambda b, j: (b, v0 // dv + j)),
            pl.BlockSpec((seq, dv), lambda b, j: (b, g0 // dv + j)),
            pl.BlockSpec((seq, LANES), lambda b, j: (b, gk_blk)),
            pl.BlockSpec((1, 2 * LANES, dk), lambda b, j: (j, 0, 0)),
            pl.BlockSpec((1, 2, dk), lambda b, j: (j, 0, 0)),
            pl.BlockSpec((1, dv), lambda b, j: (0, 0)),
        ],
        out_specs=pl.BlockSpec((seq, dv), lambda b, j: (b, j)),
        out_shape=jax.ShapeDtypeStruct((t, h * dv), BF16),
        scratch_shapes=[pltpu.VMEM((seq, dv), F32), pltpu.VMEM((2, dv, dk), F32)],
        compiler_params=_params("parallel", "parallel"),
        name="gla_scan",
    )(proj, proj, proj, proj, psmall, w2pad, gkb, nrm)


def _ln_epilogue(h, xres_ref, g_ref, b_ref, alpha, outs):
    v = _layer_norm(alpha * xres_ref[...] + h, g_ref[...], b_ref[...])
    of_ref, ob_ref, opk_ref = outs
    of_ref[...] = v
    if ob_ref is not None:
        ob_ref[...] = v.astype(BF16)
    if opk_ref is not None:
        opk_ref[...] = _pack_rows(v)


def _even_out_body(y_ref, o_ref, w_ref, xres_ref, g_ref, b_ref, of_ref, ob_ref, acc_ref, *, nk1, alpha):
    k = pl.program_id(1)

    @pl.when(k == 0)
    def _():
        acc_ref[...] = jnp.zeros_like(acc_ref)

    @pl.when(k < nk1)
    def _():
        acc_ref[...] += _dot(y_ref[...], w_ref[...])

    @pl.when(k >= nk1)
    def _():
        acc_ref[...] += _dot(o_ref[...], w_ref[...])

    @pl.when(k == pl.num_programs(1) - 1)
    def _():
        _ln_epilogue(acc_ref[...], xres_ref, g_ref, b_ref, alpha, (of_ref, ob_ref, None))


def _even_out(y, o, w, xres, g, b, alpha, tm=512, tk=1024):
    t, k1 = y.shape
    k2 = o.shape[1]
    d = w.shape[1]
    tm = min(tm, t)
    tk = min(tk, k1, k2)
    assert k1 % tk == 0 and k2 % tk == 0 and t % tm == 0
    nk1, nk2 = k1 // tk, k2 // tk
    return pl.pallas_call(
        functools.partial(_even_out_body, nk1=nk1, alpha=alpha),
        grid=(t // tm, nk1 + nk2),
        in_specs=[
            pl.BlockSpec((tm, tk), lambda i, k: (i, jnp.minimum(k, nk1 - 1))),
            pl.BlockSpec((tm, tk), lambda i, k: (i, jnp.maximum(k - nk1, 0))),
            pl.BlockSpec((tk, d), lambda i, k: (k, 0)),
            pl.BlockSpec((tm, d), lambda i, k: (i, 0)),
            pl.BlockSpec((1, d), lambda i, k: (0, 0)),
            pl.BlockSpec((1, d), lambda i, k: (0, 0)),
        ],
        out_specs=[pl.BlockSpec((tm, d), lambda i, k: (i, 0)), pl.BlockSpec((tm, d), lambda i, k: (i, 0))],
        out_shape=[jax.ShapeDtypeStruct((t, d), F32), jax.ShapeDtypeStruct((t, d), BF16)],
        scratch_shapes=[pltpu.VMEM((tm, d), F32)],
        compiler_params=_params("parallel", "arbitrary"),
        name="even_out_ln",
    )(y, o, w, xres, g, b)


def _gelu_tanh(x):
    return 0.5 * x * (1.0 + jnp.tanh(0.7978845608028654 * (x + 0.044715 * x * x * x)))


def _odd_out_body(gate_ref, hf_ref, hb_ref, w_ref, xres_ref, g_ref, b_ref, of_ref, ob_ref, acc_ref, *, alpha):
    k = pl.program_id(1)

    @pl.when(k == 0)
    def _():
        acc_ref[...] = jnp.zeros_like(acc_ref)

    hsum = hf_ref[...].astype(F32) + hb_ref[...].astype(F32)
    a = (_gelu_tanh(gate_ref[...].astype(F32)) * hsum).astype(BF16)
    acc_ref[...] += _dot(a, w_ref[...])

    @pl.when(k == pl.num_programs(1) - 1)
    def _():
        _ln_epilogue(acc_ref[...], xres_ref, g_ref, b_ref, alpha, (of_ref, ob_ref, None))


def _odd_out(proj, hf, hb, w, xres, g, b, alpha, tm=512):
    t, kw = hf.shape
    d = w.shape[1]
    tm = min(tm, t)
    nk = 2 if (kw // 2) % LANES == 0 else 1
    tk = kw // nk
    return pl.pallas_call(
        functools.partial(_odd_out_body, alpha=alpha),
        grid=(t // tm, nk),
        in_specs=[
            pl.BlockSpec((tm, tk), lambda i, k: (i, k)),
            pl.BlockSpec((tm, tk), lambda i, k: (i, k)),
            pl.BlockSpec((tm, tk), lambda i, k: (i, k)),
            pl.BlockSpec((tk, d), lambda i, k: (k, 0)),
            pl.BlockSpec((tm, d), lambda i, k: (i, 0)),
            pl.BlockSpec((1, d), lambda i, k: (0, 0)),
            pl.BlockSpec((1, d), lambda i, k: (0, 0)),
        ],
        out_specs=[pl.BlockSpec((tm, d), lambda i, k: (i, 0)), pl.BlockSpec((tm, d), lambda i, k: (i, 0))],
        out_shape=[jax.ShapeDtypeStruct((t, d), F32), jax.ShapeDtypeStruct((t, d), BF16)],
        scratch_shapes=[pltpu.VMEM((tm, d), F32)],
        compiler_params=_params("parallel", "arbitrary"),
        name="odd_out_ln",
    )(proj, hf, hb, w, xres, g, b)


def _rglru_body(vf_ref, vb_ref, wa_ref, wx_ref, ba_ref, bx_ref, lam_ref, hf_ref, hb_ref, af_ref, uf_ref, ab_ref, ub_ref, hc_ref):
    tt, width = vf_ref.shape
    nb, bw = wa_ref.shape[1], wa_ref.shape[2]

    @pl.when(pl.program_id(1) == 0)
    def _():
        hc_ref[...] = jnp.zeros_like(hc_ref)

    sp = _softplus(-lam_ref[...])
    for d, (v_ref, a_s, u_s) in enumerate(((vf_ref, af_ref, uf_ref), (vb_ref, ab_ref, ub_ref))):
        for n in range(nb):
            cs = slice(n * bw, (n + 1) * bw)
            vb16 = v_ref[:, cs]
            r = _sigmoid(_dot(vb16, wa_ref[d, n]) + ba_ref[d:d + 1, cs])
            ig = _sigmoid(_dot(vb16, wx_ref[d, n]) + bx_ref[d:d + 1, cs])
            a = jnp.exp(-RG_C * r * sp[d:d + 1, cs])
            a_s[:, cs] = a
            u_s[:, cs] = jnp.sqrt(1.0 - a * a) * (ig * vb16.astype(F32))

    def step(t8, carry):
        hf, hb = carry
        for s in range(8):
            tf = t8 * 8 + s
            tb = tt - 1 - tf
            hf = af_ref[pl.ds(tf, 1), :] * hf + uf_ref[pl.ds(tf, 1), :]
            uf_ref[pl.ds(tf, 1), :] = hf
            hb = ab_ref[pl.ds(tb, 1), :] * hb + ub_ref[pl.ds(tb, 1), :]
            ub_ref[pl.ds(tb, 1), :] = hb
        return hf, hb

    hf, hb = lax.fori_loop(0, tt // 8, step, (hc_ref[0:1, :], hc_ref[1:2, :]))
    hc_ref[0:1, :] = hf
    hc_ref[1:2, :] = hb
    hf_ref[...] = uf_ref[...].astype(hf_ref.dtype)
    hb_ref[...] = ub_ref[...].astype(hb_ref.dtype)


def _rglru(vc, wa, wx, ba, bx, lam, batch, tt=256):
    t, width = vc.shape
    seq = t // batch
    tt = min(tt, seq)
    ntt = seq // tt
    nb, bw = wa.shape[1], wa.shape[2]
    full = lambda *shape: pl.BlockSpec(shape, lambda b, j: (0,) * len(shape))
    return pl.pallas_call(
        _rglru_body,
        grid=(batch, ntt),
        in_specs=[
            pl.BlockSpec((tt, width), lambda b, j: (b * ntt + j, 0)),
            pl.BlockSpec((tt, width), lambda b, j: (b * ntt + ntt - 1 - j, 0)),
            full(2, nb, bw, bw),
            full(2, nb, bw, bw),
            full(2, width),
            full(2, width),
            full(2, width),
        ],
        out_specs=[
            pl.BlockSpec((tt, width), lambda b, j: (b * ntt + j, 0)),
            pl.BlockSpec((tt, width), lambda b, j: (b * ntt + ntt - 1 - j, 0)),
        ],
        out_shape=[jax.ShapeDtypeStruct((t, width), BF16), jax.ShapeDtypeStruct((t, width), BF16)],
        scratch_shapes=[pltpu.VMEM((tt, width), F32) for _ in range(4)] + [pltpu.VMEM((8, width), F32)],
        compiler_params=_params("parallel", "arbitrary"),
        name="rglru",
    )(vc, vc, wa, wx, ba, bx, lam)


def _xattn_body(q_ref, k_ref, v_ref, wo_ref, xres_ref, g_ref, b_ref, of_ref, opk_ref, *, alpha):
    tm, d = q_ref.shape
    hd = d // XATTN_HEADS
    scale = hd ** -0.5
    acc = jnp.zeros((tm, d), F32)
    for h in range(XATTN_HEADS):
        cs = slice(h * hd, (h + 1) * hd)
        s = _dot_nt(q_ref[:, cs], k_ref[:, cs]) * scale
        s = s - jnp.max(s, axis=-1, keepdims=True)
        e = jnp.exp(s)
        pr = (e / jnp.sum(e, axis=-1, keepdims=True)).astype(BF16)
        oh = _dot(pr, v_ref[:, cs]).astype(BF16)
        acc = acc + _dot(oh, wo_ref[cs, :])
    _ln_epilogue(acc, xres_ref, g_ref, b_ref, alpha, (of_ref, None, opk_ref))


def _xattn(q, kv, wo, xres, g, b, alpha, batch, tm=512):
    t, d = q.shape
    seq = t // batch
    mem = kv.shape[0] // batch
    tm = min(tm, seq)
    return pl.pallas_call(
        functools.partial(_xattn_body, alpha=alpha),
        grid=(t // tm,),
        in_specs=[
            pl.BlockSpec((tm, d), lambda i: (i, 0)),
            pl.BlockSpec((mem, d), lambda i: (i * tm // seq, 0)),
            pl.BlockSpec((mem, d), lambda i: (i * tm // seq, 1)),
            pl.BlockSpec((d, d), lambda i: (0, 0)),
            pl.BlockSpec((tm, d), lambda i: (i, 0)),
            pl.BlockSpec((1, d), lambda i: (0, 0)),
            pl.BlockSpec((1, d), lambda i: (0, 0)),
        ],
        out_specs=[pl.BlockSpec((tm, d), lambda i: (i, 0)), pl.BlockSpec((tm, d // 2), lambda i: (i, 0))],
        out_shape=[jax.ShapeDtypeStruct((t, d), F32), jax.ShapeDtypeStruct((t, d // 2), I32)],
        compiler_params=_params("parallel"),
        name="xattn_out_ln",
    )(q, kv, kv, wo, xres, g, b)


def _router_body(x_ref, w_ref, b_ref, ri_ref, rg_ref, cnt_ref, carry_ref):
    tr = x_ref.shape[0]

    @pl.when(pl.program_id(0) == 0)
    def _():
        carry_ref[...] = jnp.zeros_like(carry_ref)

    logits = jnp.dot(x_ref[...], w_ref[...], precision=HI, preferred_element_type=F32) + b_ref[...]
    lane = lax.broadcasted_iota(I32, (tr, LANES), 1).astype(F32)
    work = logits
    sels, vals, idxs = [], [], []
    for _ in range(TOP_K):
        m = jnp.max(work, axis=-1, keepdims=True)
        idx = jnp.min(jnp.where(work == m, lane, float(LANES)), axis=-1, keepdims=True)
        sel = lane == idx
        sels.append(sel)
        vals.append(m)
        idxs.append(idx)
        work = jnp.where(sel, NEG_BIG, work)
    exps = [jnp.exp(v - vals[0]) for v in vals]
    den = exps[0]
    for e in exps[1:]:
        den = den + e
    onehot = jnp.zeros((tr, LANES), F32)
    for sel in sels:
        onehot = onehot + sel.astype(F32)
    row = lax.broadcasted_iota(I32, (tr, tr), 0)
    col = lax.broadcasted_iota(I32, (tr, tr), 1)
    strict = (col < row).astype(BF16)
    before = _dot(strict, onehot.astype(BF16)) + carry_ref[...]
    ri = jnp.zeros((tr, LANES), F32)
    rg = jnp.zeros((tr, LANES), F32)
    for k in range(TOP_K):
        rank = jnp.sum(jnp.where(sels[k], before, 0.0), axis=-1, keepdims=True)
        ri = jnp.where(lane == float(k), idxs[k], ri)
        ri = jnp.where(lane == float(TOP_K + k), rank, ri)
        rg = jnp.where(lane == float(k), exps[k] / den, rg)
    ri_ref[...] = ri.astype(I32)
    rg_ref[...] = rg
    total = carry_ref[...] + jnp.sum(onehot, axis=0, keepdims=True)
    carry_ref[...] = total
    cnt_ref[...] = total


def _router(x, w_pad, b_pad, tr=512):
    t, d = x.shape
    tr = min(tr, t)
    return pl.pallas_call(
        _router_body,
        grid=(t // tr,),
        in_specs=[
            pl.BlockSpec((tr, d), lambda i: (i, 0)),
            pl.BlockSpec((d, LANES), lambda i: (0, 0)),
            pl.BlockSpec((1, LANES), lambda i: (0, 0)),
        ],
        out_specs=[
            pl.BlockSpec((tr, LANES), lambda i: (i, 0)),
            pl.BlockSpec((tr, LANES), lambda i: (i, 0)),
            pl.BlockSpec((1, LANES), lambda i: (0, 0)),
        ],
        out_shape=[
            jax.ShapeDtypeStruct((t, LANES), I32),
            jax.ShapeDtypeStruct((t, LANES), F32),
            jax.ShapeDtypeStruct((1, LANES), F32),
        ],
        scratch_shapes=[pltpu.VMEM((1, LANES), F32)],
        compiler_params=_params("arbitrary"),
        name="moe_router",
    )(x, w_pad, b_pad)


def _dispatch_body(pos_ref, xpk_ref, xs_in_ref, xs_ref, sem):
    del xs_in_ref
    ts = xpk_ref.shape[0]

    def issue(t, carry):
        for k in range(TOP_K):
            pltpu.make_async_copy(xpk_ref.at[pl.ds(t, 1)], xs_ref.at[pl.ds(pos_ref[0, 0, TOP_K * t + k], 1)], sem).start()
        return carry

    lax.fori_loop(0, ts, issue, 0, unroll=2)
    for k in range(TOP_K):
        pltpu.make_async_copy(xpk_ref, xs_ref.at[pl.ds(0, ts)], sem).wait()


def _dispatch(xpk, pos, n_rows, ts=256):
    t, dh = xpk.shape
    ts = min(ts, t)
    pos3 = pos.reshape(t // ts, 1, ts * TOP_K)
    xs0 = jnp.zeros((n_rows, dh), I32)
    return pl.pallas_call(
        _dispatch_body,
        grid=(t // ts,),
        in_specs=[
            pl.BlockSpec((1, 1, ts * TOP_K), lambda i: (i, 0, 0), memory_space=pltpu.SMEM),
            pl.BlockSpec((ts, dh), lambda i: (i, 0)),
            pl.BlockSpec(memory_space=pl.ANY),
        ],
        out_specs=pl.BlockSpec(memory_space=pl.ANY),
        out_shape=jax.ShapeDtypeStruct((n_rows, dh), I32),
        scratch_shapes=[pltpu.SemaphoreType.DMA(())],
        input_output_aliases={2: 0},
        compiler_params=_params("arbitrary"),
        name="moe_dispatch",
    )(pos3, xpk, xs0)


def _moe_body(te_ref, tr_ref, nu_ref, xs_ref, wg_ref, wl_ref, bg_ref, bl_ref, wd_ref, bd_ref, o_ref, acc_ref, *, sub):
    del te_ref
    i = pl.program_id(0)
    f = pl.program_id(1)
    tm, dh = xs_ref.shape

    @pl.when(i < nu_ref[0])
    def _():
        @pl.when(f == 0)
        def _():
            acc_ref[...] = jnp.broadcast_to(bd_ref[0, 0], acc_ref.shape)

        nsub = (tr_ref[i] + sub - 1) // sub

        def block(r0):
            lo, hi = _unpack_rows(xs_ref[pl.ds(r0, sub), :])
            lo = lo.astype(BF16)
            hi = hi.astype(BF16)
            hg = (_dot(lo, wg_ref[0, 0, 0:dh, :].astype(BF16)) + _dot(hi, wg_ref[0, 0, dh:2 * dh, :].astype(BF16))
                  + bg_ref[0, 0])
            hl = (_dot(lo, wl_ref[0, 0, 0:dh, :].astype(BF16)) + _dot(hi, wl_ref[0, 0, dh:2 * dh, :].astype(BF16))
                  + bl_ref[0, 0])
            glu = jnp.minimum(hg, SWIGLU_LIMIT)
            lin = jnp.clip(hl, -SWIGLU_LIMIT, SWIGLU_LIMIT)
            act = ((lin + 1.0) * glu * _sigmoid(SWIGLU_ALPHA * glu)).astype(BF16)
            acc_ref[pl.ds(r0, sub), :] += _dot(act, wd_ref[0, 0].astype(BF16))

        def pair(pi, carry):
            base = pl.multiple_of(pi * (2 * sub), 2 * sub)
            block(base)
            block(base + sub)
            return carry

        lax.fori_loop(0, nsub // 2, pair, 0)

        @pl.when(nsub % 2 == 1)
        def _():
            block(pl.multiple_of((nsub - 1) * sub, sub))

        @pl.when(f == pl.num_programs(1) - 1)
        def _():
            for s in range(tm // sub):
                o_ref[s * sub:(s + 1) * sub, :] = _pack_rows(acc_ref[s * sub:(s + 1) * sub, :])


def _moe_experts(xs, w_gu, b_gu, w_dn, b_dn, layer, tile_e, tile_rows, n_used, tm, tf, sub):
    n_rows, dh = xs.shape
    d = 2 * dh
    depth, e, _, ff2 = w_gu.shape
    ff = ff2 // 2
    tf = min(tf, ff)
    nf = ff // tf
    nt = n_rows // tm

    def tile_idx(i, f, te, tr, nu):
        return (jnp.minimum(i, nu[0] - 1), 0)

    def fidx(i, f, nu):
        return jnp.where(i < nu[0], f, nf - 1)

    grid_spec = pltpu.PrefetchScalarGridSpec(
        num_scalar_prefetch=3,
        grid=(nt, nf),
        in_specs=[
            pl.BlockSpec((tm, dh), tile_idx),
            pl.BlockSpec((1, 1, d, tf), lambda i, f, te, tr, nu: (layer, te[i], 0, fidx(i, f, nu))),
            pl.BlockSpec((1, 1, d, tf), lambda i, f, te, tr, nu: (layer, te[i], 0, nf + fidx(i, f, nu))),
            pl.BlockSpec((1, 1, 1, tf), lambda i, f, te, tr, nu: (layer, te[i], 0, fidx(i, f, nu))),
            pl.BlockSpec((1, 1, 1, tf), lambda i, f, te, tr, nu: (layer, te[i], 0, nf + fidx(i, f, nu))),
            pl.BlockSpec((1, 1, tf, d), lambda i, f, te, tr, nu: (layer, te[i], fidx(i, f, nu), 0)),
            pl.BlockSpec((1, 1, 1, d), lambda i, f, te, tr, nu: (layer, te[i], 0, 0)),
        ],
        out_specs=pl.BlockSpec((tm, dh), tile_idx),
        scratch_shapes=[pltpu.VMEM((tm, d), F32)],
    )
    return pl.pallas_call(
        functools.partial(_moe_body, sub=sub),
        grid_spec=grid_spec,
        out_shape=jax.ShapeDtypeStruct((n_rows, dh), I32),
        input_output_aliases={3: 0},
        compiler_params=_params("arbitrary", "arbitrary"),
        name="moe_experts",
    )(tile_e, tile_rows, n_used, xs, w_gu, w_gu, b_gu.reshape(depth, e, 1, ff2), b_gu.reshape(depth, e, 1, ff2), w_dn,
      b_dn.reshape(depth, e, 1, d))


def _combine_body(pos_ref, pos_next_ref, rg_ref, outs_ref, xres_ref, g_ref, b_ref, of_ref, ob_ref, buf_ref, sem, *,
                  alpha):
    i = pl.program_id(0)
    tc = rg_ref.shape[0]
    slot = i % 2

    def gather_rows(p_ref, dst_slot):
        def issue(t, carry):
            for k in range(TOP_K):
                pltpu.make_async_copy(outs_ref.at[pl.ds(p_ref[0, 0, TOP_K * t + k], 1)],
                                      buf_ref.at[dst_slot, k, pl.ds(t, 1)], sem.at[dst_slot]).start()
            return carry

        lax.fori_loop(0, tc, issue, 0, unroll=2)

    @pl.when(i == 0)
    def _():
        gather_rows(pos_ref, 0)

    @pl.when(i + 1 < pl.num_programs(0))
    def _():
        gather_rows(pos_next_ref, 1 - slot)

    for k in range(TOP_K):
        pltpu.make_async_copy(outs_ref.at[pl.ds(0, tc)], buf_ref.at[slot, k], sem.at[slot]).wait()
    gates = rg_ref[...]
    ylo = yhi = None
    for k in range(TOP_K):
        lo, hi = _unpack_rows(buf_ref[slot, k])
        gk = gates[:, k:k + 1]
        ylo = gk * lo if ylo is None else ylo + gk * lo
        yhi = gk * hi if yhi is None else yhi + gk * hi
    y = jnp.concatenate([ylo, yhi], axis=1)
    _ln_epilogue(y, xres_ref, g_ref, b_ref, alpha, (of_ref, ob_ref, None))


def _combine(outs, pos, rg, xres, g, b, alpha, tc=128):
    t, d = xres.shape
    tc = min(tc, t)
    dh = d // 2
    nsteps = t // tc
    pos3 = pos.reshape(nsteps, 1, tc * TOP_K)
    return pl.pallas_call(
        functools.partial(_combine_body, alpha=alpha),
        grid=(nsteps,),
        in_specs=[
            pl.BlockSpec((1, 1, tc * TOP_K), lambda i: (i, 0, 0), memory_space=pltpu.SMEM),
            pl.BlockSpec((1, 1, tc * TOP_K), lambda i: (jnp.minimum(i + 1, nsteps - 1), 0, 0), memory_space=pltpu.SMEM),
            pl.BlockSpec((tc, LANES), lambda i: (i, 0)),
            pl.BlockSpec(memory_space=pl.ANY),
            pl.BlockSpec((tc, d), lambda i: (i, 0)),
            pl.BlockSpec((1, d), lambda i: (0, 0)),
            pl.BlockSpec((1, d), lambda i: (0, 0)),
        ],
        out_specs=[pl.BlockSpec((tc, d), lambda i: (i, 0)), pl.BlockSpec((tc, d), lambda i: (i, 0))],
        out_shape=[jax.ShapeDtypeStruct((t, d), F32), jax.ShapeDtypeStruct((t, d), BF16)],
        scratch_shapes=[pltpu.VMEM((2, TOP_K, tc, dh), I32), pltpu.SemaphoreType.DMA((2,))],
        compiler_params=_params("arbitrary"),
        name="moe_combine_ln",
    )(pos3, pos3, rg, outs, xres, g, b)


def _moe_layer(xf, xpk, router_w, router_b, w_gu, b_gu, w_dn, b_dn, layer, g, b, alpha, tm, tf, sub):
    t, d = xf.shape
    e = router_w.shape[1]
    w_pad = jnp.zeros((d, LANES), F32).at[:, :e].set(router_w)
    b_pad = jnp.full((1, LANES), -1e30, F32).at[0, :e].set(router_b)
    ri, rg, cnt = _router(xf, w_pad, b_pad)
    counts = cnt[0, :e].astype(I32)
    nt_e = (counts + tm - 1) // tm
    per_e = ((counts + jnp.maximum(nt_e, 1) * sub - 1) // (jnp.maximum(nt_e, 1) * sub)) * sub
    per_e = jnp.maximum(per_e, sub)
    tile_end = jnp.cumsum(nt_e)
    tile_start = tile_end - nt_e
    n_used = tile_end[-1]
    nt = (t * TOP_K) // tm + e
    e_idx = ri[:, 0:TOP_K]
    rank = ri[:, TOP_K:2 * TOP_K]
    per_a = per_e[e_idx]
    tile_k = jnp.floor((rank.astype(F32) + 0.5) / per_a.astype(F32)).astype(I32)
    pos = (tile_start[e_idx] + tile_k) * tm + rank - tile_k * per_a
    tid = jnp.minimum(jnp.arange(nt, dtype=I32), n_used - 1)
    tile_e = jnp.minimum(jnp.searchsorted(tile_end, tid, side="right"), e - 1).astype(I32)
    tile_rows = jnp.clip(counts[tile_e] - (tid - tile_start[tile_e]) * per_e[tile_e], 0, per_e[tile_e]).astype(I32)
    xs = _dispatch(xpk, pos, nt * tm)
    outs = _moe_experts(xs, w_gu, b_gu, w_dn, b_dn, layer, tile_e, tile_rows, n_used.reshape(1).astype(I32), tm, tf,
                        sub)
    return _combine(outs, pos, rg, xf, g, b, alpha)


def _xattn_layer(xf, xb, mem_b, wq, wkv, wo, g, b, alpha, batch):
    q = _matmul(xb, wq.astype(BF16), BF16)
    kv = _matmul(mem_b, wkv.astype(BF16), BF16)
    return _xattn(q, kv, wo.astype(BF16), xf, g, b, alpha, batch)


def _even_mixer(xf, xb, w_in, conv_w, conv_b, a_log, dt_bias, d_skip, ssd_norm, gk_w2, gk_b, gla_norm, w_out, g, b,
                alpha, batch):
    d = xf.shape[1]
    grp = SSD_GROUPS
    heads = a_log.shape[1]
    r_heads = heads // grp
    d_ssd = heads * SSD_HEAD_DIM
    n_state = (conv_w.shape[1] - d_ssd) // (2 * grp)
    conv_ch = d_ssd + 2 * grp * n_state
    rank = gk_w2.shape[1]
    dk = gk_w2.shape[2] // GLA_HEADS
    dv = gla_norm.shape[0]
    o_xbc = d_ssd
    o_dt = o_xbc + conv_ch
    o_q = o_dt + 2 * heads
    o_k = o_q + GLA_HEADS * dk
    o_v = o_k + GLA_HEADS * dk
    o_g = o_v + GLA_HEADS * dv
    o_lr = o_g + GLA_HEADS * dv
    assert 2 * r_heads <= LANES and 2 * rank <= LANES and o_lr + 2 * rank == w_in.shape[1]
    w_big = jnp.concatenate([w_in[:, :o_dt], w_in[:, o_q:o_lr]], axis=1).astype(BF16)
    w_dt = w_in[:, o_dt:o_q].reshape(d, 2, grp, r_heads).transpose(0, 2, 1, 3).reshape(d, grp, 2 * r_heads)
    w_dt = jnp.pad(w_dt, ((0, 0), (0, 0), (0, LANES - 2 * r_heads))).reshape(d, grp * LANES)
    w_lr = jnp.pad(w_in[:, o_lr:], ((0, 0), (0, LANES - 2 * rank)))
    w_small = jnp.concatenate([w_dt, w_lr], axis=1).astype(BF16)
    proj = _matmul(xb, w_big, BF16)
    psmall = _matmul(xb, w_small, F32)

    xbc = _dwconv(proj, o_xbc, conv_ch, conv_w, conv_b, batch, act=True)

    def group_lanes(p):
        p = p.reshape(2, grp, r_heads).transpose(1, 0, 2).reshape(grp, 1, 2 * r_heads)
        return jnp.pad(p, ((0, 0), (0, 0), (0, LANES - 2 * r_heads)))

    y = _ssd(xbc, proj, psmall, group_lanes(a_log), group_lanes(dt_bias),
             jnp.repeat(d_skip, SSD_HEAD_DIM).reshape(1, d_ssd), ssd_norm.reshape(1, d_ssd), batch, d_ssd, n_state)

    w2 = gk_w2.reshape(2, rank, GLA_HEADS, dk).transpose(2, 0, 1, 3)
    w2pad = jnp.zeros((GLA_HEADS, 2, LANES, dk), F32)
    for dirn in range(2):
        w2pad = w2pad.at[:, dirn, dirn * rank:(dirn + 1) * rank, :].set(w2[:, dirn])
    w2pad = w2pad.reshape(GLA_HEADS, 2 * LANES, dk)
    gkb = gk_b.reshape(2, GLA_HEADS, dk).transpose(1, 0, 2)
    q0 = o_dt
    k0 = q0 + GLA_HEADS * dk
    v0 = k0 + GLA_HEADS * dk
    g0 = v0 + GLA_HEADS * dv
    o = _gla(proj, psmall, w2pad, gkb, gla_norm.reshape(1, dv), batch, q0, k0, v0, g0, grp, dk, dv)
    return _even_out(y, o, w_out.astype(BF16), xf, g, b, alpha)


def _odd_mixer(xf, xb, w_in, conv_w, conv_b, wa, ba, wx, bx, lam, w_out, g, b, alpha, batch):
    width = conv_w.shape[1]
    proj = _matmul(xb, w_in.astype(BF16), BF16)
    vc = _dwconv(proj, width, width, conv_w, conv_b, batch, act=False)
    hf, hb = _rglru(vc, wa.astype(BF16), wx.astype(BF16), ba, bx, lam, batch)
    return _odd_out(proj, hf, hb, w_out.astype(BF16), xf, g, b, alpha)


def kernel(x, mem, ln_g, ln_b, even_w_in, even_conv_w, even_conv_b, ssd_a_log, ssd_dt_bias, ssd_d, ssd_norm, gla_gk_w2, gla_gk_b, gla_norm, even_w_out, odd_w_in, odd_conv_w, odd_conv_b, rg_wa, rg_ba, rg_wx, rg_bx, rg_lam, odd_w_out, xattn_wq, xattn_wkv, xattn_wo, router_w, router_b, moe_w_gu, moe_b_gu, moe_w_dn, moe_b_dn):
    batch, seq, d = x.shape
    depth = ln_g.shape[0]
    alpha = float((2 * depth) ** 0.25)
    xf = x.reshape(batch * seq, d)
    xb = xf.astype(BF16)
    mem_b = mem.reshape(-1, d).astype(BF16)
    for layer in range(depth):
        i = layer // 2
        lg = lambda s: ln_g[layer, s].reshape(1, d)
        lb = lambda s: ln_b[layer, s].reshape(1, d)
        if layer % 2 == 0:
            xf, xb = _even_mixer(xf, xb, even_w_in[i], even_conv_w[i], even_conv_b[i], ssd_a_log[i], ssd_dt_bias[i],
                                 ssd_d[i], ssd_norm[i], gla_gk_w2[i], gla_gk_b[i], gla_norm[i], even_w_out[i],
                                 lg(0), lb(0), alpha, batch)
        else:
            xf, xb = _odd_mixer(xf, xb, odd_w_in[i], odd_conv_w[i], odd_conv_b[i], rg_wa[i], rg_ba[i], rg_wx[i],
                                rg_bx[i], rg_lam[i], odd_w_out[i], lg(0), lb(0), alpha, batch)
        xf, xpk = _xattn_layer(xf, xb, mem_b, xattn_wq[layer], xattn_wkv[layer], xattn_wo[layer], lg(1), lb(1), alpha,
                               batch)
        xf, xb = _moe_layer(xf, xpk, router_w[layer], router_b[layer], moe_w_gu, moe_b_gu, moe_w_dn, moe_b_dn, layer,
                            lg(2), lb(2), alpha, MOE_TILE, MOE_FF_TILE, MOE_SUB)
    return xf.reshape(batch, seq, d)
```

```python
import functools

import jax
import jax.numpy as jnp
from jax import lax
from jax.experimental import pallas as pl
from jax.experimental.pallas import tpu as pltpu

F32 = jnp.float32
BF16 = jnp.bfloat16
I32 = jnp.int32

SSD_HEAD_DIM = 64
SSD_GROUPS = 4
SSD_CHUNK = 128
GLA_HEADS = 4
GLA_NORMALIZER = 16.0
GLA_CHUNK = 64
GLA_GROUP = 4
RG_C = 8.0
XATTN_HEADS = 4
TOP_K = 4
SWIGLU_ALPHA = 1.702
SWIGLU_LIMIT = 7.0
LN_EPS = 1e-5
RMS_EPS = 1e-6

LANES = 128
VMEM_LIMIT_BYTES = 56 * 1024 * 1024
CONV_HALO = 16
MOE_TILE = 1024
MOE_FF_TILE = 512
MOE_SUB = 256
NEG_BIG = -3.0e38
HI = lax.Precision.HIGHEST

TN_DIMS = (((0,), (0,)), ((), ()))
NT_DIMS = (((1,), (1,)), ((), ()))


def _params(*sem):
    return pltpu.CompilerParams(dimension_semantics=sem, vmem_limit_bytes=VMEM_LIMIT_BYTES)


def _dot(a, b):
    return jnp.dot(a, b, preferred_element_type=F32)


def _dot_nt(a, b):
    return lax.dot_general(a, b, NT_DIMS, preferred_element_type=F32)


def _dot_tn(a, b):
    return lax.dot_general(a, b, TN_DIMS, preferred_element_type=F32)


def _sigmoid(x):
    return 1.0 / (1.0 + jnp.exp(-x))


def _softplus(x):
    return jnp.maximum(x, 0.0) + jnp.log(1.0 + jnp.exp(-jnp.abs(x)))


def _split3(x):
    hi = x.astype(BF16)
    rem = x - hi.astype(F32)
    mid = rem.astype(BF16)
    lo = (rem - mid.astype(F32)).astype(BF16)
    return hi, mid, lo


def _tri_cumsum(tri, x):
    hi, mid, lo = _split3(x)
    return _dot(tri, hi) + _dot(tri, mid) + _dot(tri, lo)


def _tri_cumsum_tn(x, tri_t):
    hi, mid, lo = _split3(x)
    return _dot_tn(hi, tri_t) + _dot_tn(mid, tri_t) + _dot_tn(lo, tri_t)


def _layer_norm(v, g, b):
    mu = jnp.mean(v, axis=-1, keepdims=True)
    c = v - mu
    var = jnp.mean(c * c, axis=-1, keepdims=True)
    return c * lax.rsqrt(var + LN_EPS) * g + b


def _pack_rows(v):
    h = v.shape[1] // 2
    lo = pltpu.bitcast(v[:, :h].astype(BF16).astype(F32), I32)
    hi = pltpu.bitcast(v[:, h:].astype(BF16).astype(F32), I32)
    return (lo & jnp.int32(-65536)) | (lax.shift_right_logical(hi, jnp.int32(16)))


def _unpack_rows(w):
    lo = pltpu.bitcast(w & jnp.int32(-65536), F32)
    hi = pltpu.bitcast(lax.shift_left(w, jnp.int32(16)), F32)
    return lo, hi


def _largest_tile(n, cap, *also):
    for c in range(min(cap, n) // LANES * LANES, 0, -LANES):
        if n % c == 0 and all(a % c == 0 for a in also):
            return c
    return n


def _mm_body(a_ref, w_ref, o_ref):
    o_ref[...] = _dot(a_ref[...], w_ref[...]).astype(o_ref.dtype)


def _matmul(a, w, out_dtype, tm=1024, tn=1024):
    m, k = a.shape
    n = w.shape[1]
    tm = min(tm, m)
    tn = _largest_tile(n, tn)
    assert m % tm == 0
    return pl.pallas_call(
        _mm_body,
        grid=(n // tn, m // tm),
        in_specs=[pl.BlockSpec((tm, k), lambda j, i: (i, 0)), pl.BlockSpec((k, tn), lambda j, i: (0, j))],
        out_specs=pl.BlockSpec((tm, tn), lambda j, i: (i, j)),
        out_shape=jax.ShapeDtypeStruct((m, n), out_dtype),
        compiler_params=_params("parallel", "parallel"),
        name="proj_matmul",
    )(a, w)


def _conv_body(x_ref, w_ref, b_ref, o_ref, *, act, rows):
    seq = x_ref.shape[0]
    nchunks = seq // rows
    tot = rows + 2 * CONV_HALO
    w = w_ref[...]
    b = b_ref[...]

    def body(i, carry):
        r0 = pl.multiple_of(i * rows, rows)
        main = x_ref[pl.ds(r0, rows), :].astype(F32)
        p0 = pl.multiple_of(jnp.maximum(r0 - CONV_HALO, 0), CONV_HALO)
        prev = x_ref[pl.ds(p0, CONV_HALO), :].astype(F32)
        prev = jnp.where(i > 0, prev, 0.0)
        n0 = pl.multiple_of(jnp.minimum(r0 + rows, seq - CONV_HALO), CONV_HALO)
        nxt = x_ref[pl.ds(n0, CONV_HALO), :].astype(F32)
        nxt = jnp.where(i < nchunks - 1, nxt, 0.0)
        ext = jnp.concatenate([prev, main, nxt], axis=0)
        acc = b + w[1:2] * main
        for j in (0, 2, 3):
            shift = (1 - j) % tot
            acc = acc + w[j:j + 1] * pltpu.roll(ext, shift, 0)[CONV_HALO:CONV_HALO + rows]
        if act:
            acc = acc * _sigmoid(acc)
        o_ref[pl.ds(r0, rows), :] = acc.astype(o_ref.dtype)
        return carry

    lax.fori_loop(0, nchunks, body, 0)


def _dwconv(x, col0, width, conv_w, conv_b, batch, act, cw=512):
    t = x.shape[0]
    seq = t // batch
    cw = _largest_tile(width, cw, col0)
    assert width % cw == 0 and col0 % cw == 0
    rows = min(256, seq)
    cb0 = col0 // cw
    return pl.pallas_call(
        functools.partial(_conv_body, act=act, rows=rows),
        grid=(batch, width // cw),
        in_specs=[
            pl.BlockSpec((seq, cw), lambda b, j: (b, cb0 + j)),
            pl.BlockSpec((conv_w.shape[0], cw), lambda b, j: (0, j)),
            pl.BlockSpec((1, cw), lambda b, j: (0, j)),
        ],
        out_specs=pl.BlockSpec((seq, cw), lambda b, j: (b, j)),
        out_shape=jax.ShapeDtypeStruct((t, width), BF16),
        compiler_params=_params("parallel", "parallel"),
        name="dwconv",
    )(x, conv_w, conv_b.reshape(1, -1))


def _ssd_body(xs_ref, b_ref, c_ref, z_ref, ps_ref, alog_ref, dtb_ref, dsk_ref, nrm_ref, y_ref, yacc_ref, h_ref):
    seq, hp = xs_ref.shape
    q = SSD_CHUNK
    nc = seq // q
    p = SSD_HEAD_DIM
    r_heads = hp // p
    a_neg = -jnp.exp(alog_ref[0])
    dtb = dtb_ref[0]
    row = lax.broadcasted_iota(I32, (q, q), 0)
    col = lax.broadcasted_iota(I32, (q, q), 1)
    lower = (col <= row).astype(BF16)
    upper = (col >= row).astype(BF16)
    e_row = lax.broadcasted_iota(I32, (LANES, hp), 0)
    e_col = lax.broadcasted_iota(I32, (LANES, hp), 1) // p
    pw = min(LANES, hp)
    piece_head = lax.broadcasted_iota(I32, (q, pw), 1) // p

    h_ref[...] = jnp.zeros_like(h_ref)

    def chunk_dir(c, dirn, final):
        tri, tri_t = (lower, upper) if dirn == 0 else (upper, lower)
        mask = (col <= row) if dirn == 0 else (col >= row)
        expand = (e_row == e_col + dirn * r_heads).astype(BF16)
        edge = q - 1 if dirn == 0 else 0
        r0 = pl.multiple_of(c * q, q)
        x = xs_ref[pl.ds(r0, q), :]
        xf = x.astype(F32)
        bm = b_ref[pl.ds(r0, q), :]
        cm = c_ref[pl.ds(r0, q), :]
        dt = _softplus(ps_ref[pl.ds(r0, q), :] + dtb)
        dta = dt * a_neg
        acum = _tri_cumsum(tri, dta)
        acum_t = _tri_cumsum_tn(dta, tri_t)
        alast = acum[edge:edge + 1, :]
        stack = jnp.concatenate([dt, jnp.exp(acum), dt * jnp.exp(alast - acum)], axis=0).astype(BF16)
        ex = _dot(stack, expand)
        xdt = (xf * ex[0:q]).astype(BF16)
        eac = ex[q:2 * q]
        xw = (xf * ex[2 * q:3 * q]).astype(BF16)
        cb = _dot_nt(cm, bm)
        yd = []
        for j in range(hp // pw):
            xpiece = xdt[:, j * pw:(j + 1) * pw]
            ypiece = None
            for a in range(pw // p):
                cr = dirn * r_heads + j * (pw // p) + a
                seg = acum[:, cr:cr + 1] - acum_t[cr:cr + 1, :]
                dec = jnp.where(mask, jnp.exp(jnp.minimum(seg, 0.0)), 0.0)
                wgt = (cb * dec).astype(BF16)
                xa = xpiece if pw == p else jnp.where(piece_head == a, xpiece, jnp.zeros_like(xpiece))
                ya = _dot(wgt, xa)
                ypiece = ya if ypiece is None else ypiece + ya
            yd.append(ypiece)
        y = jnp.concatenate(yd, axis=1) if len(yd) > 1 else yd[0]
        hst = h_ref[dirn]
        y = y + _dot(cm, hst.astype(BF16)) * eac
        ea = jnp.broadcast_to(jnp.exp(alast), (8, LANES))
        ea_hi = ea.astype(BF16)
        ea_lo = (ea - ea_hi.astype(F32)).astype(BF16)
        hdec = (_dot(ea_hi, expand) + _dot(ea_lo, expand))[0:1]
        h_ref[dirn] = hst * hdec + _dot_tn(bm, xw)
        if not final:
            yacc_ref[pl.ds(r0, q), :] = y
        else:
            yt = yacc_ref[pl.ds(r0, q), :] + y + dsk_ref[...] * xf
            zf = z_ref[pl.ds(r0, q), :].astype(F32)
            yt = yt * (zf * _sigmoid(zf))
            ms = jnp.mean(yt * yt, axis=-1, keepdims=True)
            y_ref[pl.ds(r0, q), :] = (yt * lax.rsqrt(ms + RMS_EPS) * nrm_ref[...]).astype(y_ref.dtype)

    def both_dirs(final):
        def body(i, carry):
            chunk_dir(i, 0, final)
            chunk_dir(nc - 1 - i, 1, final)
            return carry
        return body

    assert nc % 2 == 0
    lax.fori_loop(0, nc // 2, both_dirs(False), 0)
    lax.fori_loop(nc // 2, nc, both_dirs(True), 0)


def _ssd(xbc, proj, psmall, alog, dtb, dskip, nrm, batch, d_ssd, n_state):
    t = xbc.shape[0]
    seq = t // batch
    g = SSD_GROUPS
    hp = d_ssd // g
    nb0 = d_ssd // n_state
    return pl.pallas_call(
        _ssd_body,
        grid=(batch, g),
        in_specs=[
            pl.BlockSpec((seq, hp), lambda b, j: (b, j)),
            pl.BlockSpec((seq, n_state), lambda b, j: (b, nb0 + j)),
            pl.BlockSpec((seq, n_state), lambda b, j: (b, nb0 + g + j)),
            pl.BlockSpec((seq, hp), lambda b, j: (b, j)),
            pl.BlockSpec((seq, LANES), lambda b, j: (b, j)),
            pl.BlockSpec((1, 1, LANES), lambda b, j: (j, 0, 0)),
            pl.BlockSpec((1, 1, LANES), lambda b, j: (j, 0, 0)),
            pl.BlockSpec((1, hp), lambda b, j: (0, j)),
            pl.BlockSpec((1, hp), lambda b, j: (0, j)),
        ],
        out_specs=pl.BlockSpec((seq, hp), lambda b, j: (b, j)),
        out_shape=jax.ShapeDtypeStruct((t, d_ssd), BF16),
        scratch_shapes=[pltpu.VMEM((seq, hp), F32), pltpu.VMEM((2, n_state, hp), F32)],
        compiler_params=_params("parallel", "parallel"),
        name="ssd_scan",
    )(xbc, xbc, xbc, proj, psmall, alog, dtb, dskip, nrm)


def _gla_body(q_ref, k_ref, v_ref, go_ref, ps_ref, w2_ref, gkb_ref, nrm_ref, o_ref, oacc_ref, s_ref):
    seq, dk = q_ref.shape
    dv = v_ref.shape[1]
    cq = GLA_CHUNK
    gq = GLA_GROUP * cq
    ng = seq // gq
    scale = dk ** -0.5
    row = lax.broadcasted_iota(I32, (gq, gq), 0)
    col = lax.broadcasted_iota(I32, (gq, gq), 1)
    same_chunk = (row // cq) == (col // cq)
    masks = (same_chunk & (col <= row), same_chunk & (col >= row))
    tris = tuple(m.astype(BF16) for m in masks)

    s_ref[...] = jnp.zeros_like(s_ref)

    def group_dir(gi, dirn, final):
        w2 = w2_ref[0, dirn * LANES:(dirn + 1) * LANES, :].astype(BF16)
        gb = gkb_ref[0, dirn:dirn + 1, :]
        edge = cq - 1 if dirn == 0 else 0
        r0 = pl.multiple_of(gi * gq, gq)
        lr = ps_ref[pl.ds(r0, gq), :].astype(BF16)
        gl = _dot(lr, w2) + gb
        g = -_softplus(-gl) * (1.0 / GLA_NORMALIZER)
        bcum = _tri_cumsum(tris[dirn], g)
        qf = q_ref[pl.ds(r0, gq), :].astype(F32) * scale
        kf = k_ref[pl.ds(r0, gq), :].astype(F32)
        v = v_ref[pl.ds(r0, gq), :]
        q_in = (qf * jnp.exp(bcum)).astype(BF16)
        k_in = (kf * jnp.exp(-bcum)).astype(BF16)
        att = jnp.where(masks[dirn], _dot_nt(q_in, k_in), 0.0).astype(BF16)
        o_intra = _dot(att, v)
        order = range(GLA_GROUP) if dirn == 0 else range(GLA_GROUP - 1, -1, -1)
        for j in order:
            rows = slice(j * cq, (j + 1) * cq)
            blast = bcum[j * cq + edge:j * cq + edge + 1, :]
            k_st = (kf[rows] * jnp.exp(blast - bcum[rows])).astype(BF16)
            st = s_ref[dirn]
            o = o_intra[rows] + _dot_nt(q_in[rows], st.astype(BF16))
            s_ref[dirn] = st * jnp.exp(blast) + _dot_tn(v[rows], k_st)
            rj = pl.multiple_of(r0 + j * cq, cq)
            if not final:
                oacc_ref[pl.ds(rj, cq), :] = o
            else:
                ot = oacc_ref[pl.ds(rj, cq), :] + o
                ms = jnp.mean(ot * ot, axis=-1, keepdims=True)
                gf = go_ref[pl.ds(rj, cq), :].astype(F32)
                o_ref[pl.ds(rj, cq), :] = (ot * lax.rsqrt(ms + RMS_EPS) * nrm_ref[...] * (gf * _sigmoid(gf))).astype(o_ref.dtype)

    def both_dirs(final):
        def body(i, carry):
            group_dir(i, 0, final)
            group_dir(ng - 1 - i, 1, final)
            return carry
        return body

    assert ng % 2 == 0
    lax.fori_loop(0, ng // 2, both_dirs(False), 0)
    lax.fori_loop(ng // 2, ng, both_dirs(True), 0)


def _gla(proj, psmall, w2pad, gkb, nrm, batch, q0, k0, v0, g0, gk_blk, dk, dv):
    t = proj.shape[0]
    seq = t // batch
    h = GLA_HEADS
    return pl.pallas_call(
        _gla_body,
        grid=(batch, h),
        in_specs=[
            pl.BlockSpec((seq, dk), lambda b, j: (b, q0 // dk + j)),
            pl.BlockSpec((seq, dk), lambda b, j: (b, k0 // dk + j)),
            pl.BlockSpec((seq, dv), lambda b, j: (b, v0 // dv + j)),
            pl.BlockSpec((seq, dv), lambda b, j: (b, g0 // dv + j)),
            pl.BlockSpec((seq, LANES), lambda b, j: (b, gk_blk)),
            pl.BlockSpec((1, 2 * LANES, dk), lambda b, j: (j, 0, 0)),
            pl.BlockSpec((1, 2, dk), lambda b, j: (j, 0, 0)),
            pl.BlockSpec((1, dv), lambda b, j: (0, 0)),
        ],
        out_specs=pl.BlockSpec((seq, dv), lambda b, j: (b, j)),
        out_shape=jax.ShapeDtypeStruct((t, h * dv), BF16),
        scratch_shapes=[pltpu.VMEM((seq, dv), F32), pltpu.VMEM((2, dv, dk), F32)],
        compiler_params=_params("parallel", "parallel"),
        name="gla_scan",
    )(proj, proj, proj, proj, psmall, w2pad, gkb, nrm)


def _ln_epilogue(h, xres_ref, g_ref, b_ref, alpha, outs):
    v = _layer_norm(alpha * xres_ref[...] + h, g_ref[...], b_ref[...])
    of_ref, ob_ref, opk_ref = outs
    of_ref[...] = v
    if ob_ref is not None:
        ob_ref[...] = v.astype(BF16)
    if opk_ref is not None:
        opk_ref[...] = _pack_rows(v)


def _even_out_body(y_ref, o_ref, w_ref, xres_ref, g_ref, b_ref, of_ref, ob_ref, *, alpha):
    k1 = y_ref.shape[1]
    h = _dot(y_ref[...], w_ref[0:k1, :]) + _dot(o_ref[...], w_ref[k1:, :])
    _ln_epilogue(h, xres_ref, g_ref, b_ref, alpha, (of_ref, ob_ref, None))


def _even_out(y, o, w, xres, g, b, alpha, tm=512):
    t, k1 = y.shape
    k2 = o.shape[1]
    d = w.shape[1]
    tm = min(tm, t)
    assert t % tm == 0
    return pl.pallas_call(
        functools.partial(_even_out_body, alpha=alpha),
        grid=(t // tm,),
        in_specs=[
            pl.BlockSpec((tm, k1), lambda i: (i, 0)),
            pl.BlockSpec((tm, k2), lambda i: (i, 0)),
            pl.BlockSpec((k1 + k2, d), lambda i: (0, 0), pipeline_mode=pl.Buffered(1)),
            pl.BlockSpec((tm, d), lambda i: (i, 0)),
            pl.BlockSpec((1, d), lambda i: (0, 0)),
            pl.BlockSpec((1, d), lambda i: (0, 0)),
        ],
        out_specs=[pl.BlockSpec((tm, d), lambda i: (i, 0)), pl.BlockSpec((tm, d), lambda i: (i, 0))],
        out_shape=[jax.ShapeDtypeStruct((t, d), F32), jax.ShapeDtypeStruct((t, d), BF16)],
        compiler_params=_params("parallel"),
        name="even_out_ln",
    )(y, o, w, xres, g, b)


def _gelu_tanh(x):
    return 0.5 * x * (1.0 + jnp.tanh(0.7978845608028654 * (x + 0.044715 * x * x * x)))


def _odd_out_body(gate_ref, hf_ref, hb_ref, w_ref, xres_ref, g_ref, b_ref, of_ref, ob_ref, *, alpha):
    hsum = hf_ref[...].astype(F32) + hb_ref[...].astype(F32)
    a = (_gelu_tanh(gate_ref[...].astype(F32)) * hsum).astype(BF16)
    _ln_epilogue(_dot(a, w_ref[...]), xres_ref, g_ref, b_ref, alpha, (of_ref, ob_ref, None))


def _odd_out(proj, hf, hb, w, xres, g, b, alpha, tm=512):
    t, kw = hf.shape
    d = w.shape[1]
    tm = min(tm, t)
    return pl.pallas_call(
        functools.partial(_odd_out_body, alpha=alpha),
        grid=(t // tm,),
        in_specs=[
            pl.BlockSpec((tm, kw), lambda i: (i, 0)),
            pl.BlockSpec((tm, kw), lambda i: (i, 0)),
            pl.BlockSpec((tm, kw), lambda i: (i, 0)),
            pl.BlockSpec((kw, d), lambda i: (0, 0), pipeline_mode=pl.Buffered(1)),
            pl.BlockSpec((tm, d), lambda i: (i, 0)),
            pl.BlockSpec((1, d), lambda i: (0, 0)),
            pl.BlockSpec((1, d), lambda i: (0, 0)),
        ],
        out_specs=[pl.BlockSpec((tm, d), lambda i: (i, 0)), pl.BlockSpec((tm, d), lambda i: (i, 0))],
        out_shape=[jax.ShapeDtypeStruct((t, d), F32), jax.ShapeDtypeStruct((t, d), BF16)],
        compiler_params=_params("parallel"),
        name="odd_out_ln",
    )(proj, hf, hb, w, xres, g, b)


def _rglru_body(vf_ref, vb_ref, wa_ref, wx_ref, ba_ref, bx_ref, lam_ref, hf_ref, hb_ref, af_ref, uf_ref, ab_ref, ub_ref, hc_ref):
    tt, width = vf_ref.shape
    nb, bw = wa_ref.shape[1], wa_ref.shape[2]

    @pl.when(pl.program_id(1) == 0)
    def _():
        hc_ref[...] = jnp.zeros_like(hc_ref)

    sp = _softplus(-lam_ref[...])
    for d, (v_ref, a_s, u_s) in enumerate(((vf_ref, af_ref, uf_ref), (vb_ref, ab_ref, ub_ref))):
        for n in range(nb):
            cs = slice(n * bw, (n + 1) * bw)
            vb16 = v_ref[:, cs]
            r = _sigmoid(_dot(vb16, wa_ref[d, n]) + ba_ref[d:d + 1, cs])
            ig = _sigmoid(_dot(vb16, wx_ref[d, n]) + bx_ref[d:d + 1, cs])
            a = jnp.exp(-RG_C * r * sp[d:d + 1, cs])
            a_s[:, cs] = a
            u_s[:, cs] = jnp.sqrt(1.0 - a * a) * (ig * vb16.astype(F32))

    def step(t8, carry):
        hf, hb = carry
        for s in range(8):
            tf = t8 * 8 + s
            tb = tt - 1 - tf
            hf = af_ref[pl.ds(tf, 1), :] * hf + uf_ref[pl.ds(tf, 1), :]
            uf_ref[pl.ds(tf, 1), :] = hf
            hb = ab_ref[pl.ds(tb, 1), :] * hb + ub_ref[pl.ds(tb, 1), :]
            ub_ref[pl.ds(tb, 1), :] = hb
        return hf, hb

    hf, hb = lax.fori_loop(0, tt // 8, step, (hc_ref[0:1, :], hc_ref[1:2, :]))
    hc_ref[0:1, :] = hf
    hc_ref[1:2, :] = hb
    hf_ref[...] = uf_ref[...].astype(hf_ref.dtype)
    hb_ref[...] = ub_ref[...].astype(hb_ref.dtype)


def _rglru(vc, wa, wx, ba, bx, lam, batch, tt=256):
    t, width = vc.shape
    seq = t // batch
    tt = min(tt, seq)
    ntt = seq // tt
    nb, bw = wa.shape[1], wa.shape[2]
    full = lambda *shape: pl.BlockSpec(shape, lambda b, j: (0,) * len(shape))
    return pl.pallas_call(
        _rglru_body,
        grid=(batch, ntt),
        in_specs=[
            pl.BlockSpec((tt, width), lambda b, j: (b * ntt + j, 0)),
            pl.BlockSpec((tt, width), lambda b, j: (b * ntt + ntt - 1 - j, 0)),
            full(2, nb, bw, bw),
            full(2, nb, bw, bw),
            full(2, width),
            full(2, width),
            full(2, width),
        ],
        out_specs=[
            pl.BlockSpec((tt, width), lambda b, j: (b * ntt + j, 0)),
            pl.BlockSpec((tt, width), lambda b, j: (b * ntt + ntt - 1 - j, 0)),
        ],
        out_shape=[jax.ShapeDtypeStruct((t, width), BF16), jax.ShapeDtypeStruct((t, width), BF16)],
        scratch_shapes=[pltpu.VMEM((tt, width), F32) for _ in range(4)] + [pltpu.VMEM((8, width), F32)],
        compiler_params=_params("parallel", "arbitrary"),
        name="rglru",
    )(vc, vc, wa, wx, ba, bx, lam)


def _xattn_body(q_ref, k_ref, v_ref, wo_ref, xres_ref, g_ref, b_ref, of_ref, opk_ref, *, alpha):
    tm, d = q_ref.shape
    hd = d // XATTN_HEADS
    scale = hd ** -0.5
    acc = jnp.zeros((tm, d), F32)
    for h in range(XATTN_HEADS):
        cs = slice(h * hd, (h + 1) * hd)
        s = _dot_nt(q_ref[:, cs], k_ref[:, cs]) * scale
        s = s - jnp.max(s, axis=-1, keepdims=True)
        e = jnp.exp(s)
        pr = (e / jnp.sum(e, axis=-1, keepdims=True)).astype(BF16)
        oh = _dot(pr, v_ref[:, cs]).astype(BF16)
        acc = acc + _dot(oh, wo_ref[cs, :])
    _ln_epilogue(acc, xres_ref, g_ref, b_ref, alpha, (of_ref, None, opk_ref))


def _xattn(q, kv, wo, xres, g, b, alpha, batch, tm=512):
    t, d = q.shape
    seq = t // batch
    mem = kv.shape[0] // batch
    tm = min(tm, seq)
    return pl.pallas_call(
        functools.partial(_xattn_body, alpha=alpha),
        grid=(t // tm,),
        in_specs=[
            pl.BlockSpec((tm, d), lambda i: (i, 0)),
            pl.BlockSpec((mem, d), lambda i: (i * tm // seq, 0)),
            pl.BlockSpec((mem, d), lambda i: (i * tm // seq, 1)),
            pl.BlockSpec((d, d), lambda i: (0, 0)),
            pl.BlockSpec((tm, d), lambda i: (i, 0)),
            pl.BlockSpec((1, d), lambda i: (0, 0)),
            pl.BlockSpec((1, d), lambda i: (0, 0)),
        ],
        out_specs=[pl.BlockSpec((tm, d), lambda i: (i, 0)), pl.BlockSpec((tm, d // 2), lambda i: (i, 0))],
        out_shape=[jax.ShapeDtypeStruct((t, d), F32), jax.ShapeDtypeStruct((t, d // 2), I32)],
        compiler_params=_params("parallel"),
        name="xattn_out_ln",
    )(q, kv, kv, wo, xres, g, b)


def _router_body(x_ref, w_ref, b_ref, ri_ref, rg_ref, cnt_ref, carry_ref):
    tr = x_ref.shape[0]

    @pl.when(pl.program_id(0) == 0)
    def _():
        carry_ref[...] = jnp.zeros_like(carry_ref)

    logits = jnp.dot(x_ref[...], w_ref[...], precision=HI, preferred_element_type=F32) + b_ref[...]
    lane = lax.broadcasted_iota(I32, (tr, LANES), 1).astype(F32)
    work = logits
    sels, vals, idxs = [], [], []
    for _ in range(TOP_K):
        m = jnp.max(work, axis=-1, keepdims=True)
        idx = jnp.min(jnp.where(work == m, lane, float(LANES)), axis=-1, keepdims=True)
        sel = lane == idx
        sels.append(sel)
        vals.append(m)
        idxs.append(idx)
        work = jnp.where(sel, NEG_BIG, work)
    exps = [jnp.exp(v - vals[0]) for v in vals]
    den = exps[0]
    for e in exps[1:]:
        den = den + e
    onehot = jnp.zeros((tr, LANES), F32)
    for sel in sels:
        onehot = onehot + sel.astype(F32)
    row = lax.broadcasted_iota(I32, (tr, tr), 0)
    col = lax.broadcasted_iota(I32, (tr, tr), 1)
    strict = (col < row).astype(BF16)
    before = _dot(strict, onehot.astype(BF16)) + carry_ref[...]
    ri = jnp.zeros((tr, LANES), F32)
    rg = jnp.zeros((tr, LANES), F32)
    for k in range(TOP_K):
        rank = jnp.sum(jnp.where(sels[k], before, 0.0), axis=-1, keepdims=True)
        ri = jnp.where(lane == float(k), idxs[k], ri)
        ri = jnp.where(lane == float(TOP_K + k), rank, ri)
        rg = jnp.where(lane == float(k), exps[k] / den, rg)
    ri_ref[...] = ri.astype(I32)
    rg_ref[...] = rg
    total = carry_ref[...] + jnp.sum(onehot, axis=0, keepdims=True)
    carry_ref[...] = total
    cnt_ref[...] = total


def _router(x, w_pad, b_pad, tr=512):
    t, d = x.shape
    tr = min(tr, t)
    return pl.pallas_call(
        _router_body,
        grid=(t // tr,),
        in_specs=[
            pl.BlockSpec((tr, d), lambda i: (i, 0)),
            pl.BlockSpec((d, LANES), lambda i: (0, 0)),
            pl.BlockSpec((1, LANES), lambda i: (0, 0)),
        ],
        out_specs=[
            pl.BlockSpec((tr, LANES), lambda i: (i, 0)),
            pl.BlockSpec((tr, LANES), lambda i: (i, 0)),
            pl.BlockSpec((1, LANES), lambda i: (0, 0)),
        ],
        out_shape=[
            jax.ShapeDtypeStruct((t, LANES), I32),
            jax.ShapeDtypeStruct((t, LANES), F32),
            jax.ShapeDtypeStruct((1, LANES), F32),
        ],
        scratch_shapes=[pltpu.VMEM((1, LANES), F32)],
        compiler_params=_params("arbitrary"),
        name="moe_router",
    )(x, w_pad, b_pad)


def _dispatch_body(pos_ref, xpk_ref, xs_in_ref, xs_ref, sem):
    del xs_in_ref
    ts = xpk_ref.shape[0]

    def issue(t, carry):
        for k in range(TOP_K):
            pltpu.make_async_copy(xpk_ref.at[pl.ds(t, 1)], xs_ref.at[pl.ds(pos_ref[0, 0, TOP_K * t + k], 1)], sem).start()
        return carry

    lax.fori_loop(0, ts, issue, 0, unroll=2)
    for k in range(TOP_K):
        pltpu.make_async_copy(xpk_ref, xs_ref.at[pl.ds(0, ts)], sem).wait()


def _dispatch(xpk, pos, n_rows, ts=256):
    t, dh = xpk.shape
    ts = min(ts, t)
    pos3 = pos.reshape(t // ts, 1, ts * TOP_K)
    xs0 = jnp.zeros((n_rows, dh), I32)
    return pl.pallas_call(
        _dispatch_body,
        grid=(t // ts,),
        in_specs=[
            pl.BlockSpec((1, 1, ts * TOP_K), lambda i: (i, 0, 0), memory_space=pltpu.SMEM),
            pl.BlockSpec((ts, dh), lambda i: (i, 0)),
            pl.BlockSpec(memory_space=pl.ANY),
        ],
        out_specs=pl.BlockSpec(memory_space=pl.ANY),
        out_shape=jax.ShapeDtypeStruct((n_rows, dh), I32),
        scratch_shapes=[pltpu.SemaphoreType.DMA(())],
        input_output_aliases={2: 0},
        compiler_params=_params("arbitrary"),
        name="moe_dispatch",
    )(pos3, xpk, xs0)


def _moe_body(te_ref, tr_ref, nu_ref, xs_ref, wg_ref, wl_ref, bg_ref, bl_ref, wd_ref, bd_ref, o_ref, acc_ref, *, sub):
    del te_ref
    i = pl.program_id(0)
    f = pl.program_id(1)
    tm, dh = xs_ref.shape

    @pl.when(i < nu_ref[0])
    def _():
        @pl.when(f == 0)
        def _():
            acc_ref[...] = jnp.broadcast_to(bd_ref[0, 0], acc_ref.shape)

        nsub = (tr_ref[i] + sub - 1) // sub

        def block(r0, m):
            lo, hi = _unpack_rows(xs_ref[pl.ds(r0, m), :])
            lo = lo.astype(BF16)
            hi = hi.astype(BF16)
            hg = (_dot(lo, wg_ref[0, 0, 0:dh, :].astype(BF16)) + _dot(hi, wg_ref[0, 0, dh:2 * dh, :].astype(BF16))
                  + bg_ref[0, 0])
            hl = (_dot(lo, wl_ref[0, 0, 0:dh, :].astype(BF16)) + _dot(hi, wl_ref[0, 0, dh:2 * dh, :].astype(BF16))
                  + bl_ref[0, 0])
            glu = jnp.minimum(hg, SWIGLU_LIMIT)
            lin = jnp.clip(hl, -SWIGLU_LIMIT, SWIGLU_LIMIT)
            act = ((lin + 1.0) * glu * _sigmoid(SWIGLU_ALPHA * glu)).astype(BF16)
            acc_ref[pl.ds(r0, m), :] += _dot(act, wd_ref[0, 0].astype(BF16))

        def double_block(pi, carry):
            block(pl.multiple_of(pi * (2 * sub), 2 * sub), 2 * sub)
            return carry

        lax.fori_loop(0, nsub // 2, double_block, 0)

        @pl.when(nsub % 2 == 1)
        def _():
            block(pl.multiple_of((nsub - 1) * sub, sub), sub)

        @pl.when(f == pl.num_programs(1) - 1)
        def _():
            for s in range(tm // sub):
                o_ref[s * sub:(s + 1) * sub, :] = _pack_rows(acc_ref[s * sub:(s + 1) * sub, :])


def _moe_experts(xs, w_gu, b_gu, w_dn, b_dn, layer, tile_e, tile_rows, n_used, tm, tf, sub):
    n_rows, dh = xs.shape
    d = 2 * dh
    depth, e, _, ff2 = w_gu.shape
    ff = ff2 // 2
    tf = min(tf, ff)
    nf = ff // tf
    nt = n_rows // tm

    def tile_idx(i, f, te, tr, nu):
        return (jnp.minimum(i, nu[0] - 1), 0)

    def fidx(i, f, nu):
        return jnp.where(i < nu[0], f, nf - 1)

    grid_spec = pltpu.PrefetchScalarGridSpec(
        num_scalar_prefetch=3,
        grid=(nt, nf),
        in_specs=[
            pl.BlockSpec((tm, dh), tile_idx),
            pl.BlockSpec((1, 1, d, tf), lambda i, f, te, tr, nu: (layer, te[i], 0, fidx(i, f, nu))),
            pl.BlockSpec((1, 1, d, tf), lambda i, f, te, tr, nu: (layer, te[i], 0, nf + fidx(i, f, nu))),
            pl.BlockSpec((1, 1, 1, tf), lambda i, f, te, tr, nu: (layer, te[i], 0, fidx(i, f, nu))),
            pl.BlockSpec((1, 1, 1, tf), lambda i, f, te, tr, nu: (layer, te[i], 0, nf + fidx(i, f, nu))),
            pl.BlockSpec((1, 1, tf, d), lambda i, f, te, tr, nu: (layer, te[i], fidx(i, f, nu), 0)),
            pl.BlockSpec((1, 1, 1, d), lambda i, f, te, tr, nu: (layer, te[i], 0, 0)),
        ],
        out_specs=pl.BlockSpec((tm, dh), tile_idx),
        scratch_shapes=[pltpu.VMEM((tm, d), F32)],
    )
    return pl.pallas_call(
        functools.partial(_moe_body, sub=sub),
        grid_spec=grid_spec,
        out_shape=jax.ShapeDtypeStruct((n_rows, dh), I32),
        input_output_aliases={3: 0},
        compiler_params=_params("arbitrary", "arbitrary"),
        name="moe_experts",
    )(tile_e, tile_rows, n_used, xs, w_gu, w_gu, b_gu.reshape(depth, e, 1, ff2), b_gu.reshape(depth, e, 1, ff2), w_dn,
      b_dn.reshape(depth, e, 1, d))


def _combine_body(pos_ref, pos_next_ref, rg_ref, outs_ref, xres_ref, g_ref, b_ref, of_ref, ob_ref, buf_ref, sem, *,
                  alpha):
    i = pl.program_id(0)
    tc = rg_ref.shape[0]
    slot = i % 2

    def gather_rows(p_ref, dst_slot):
        def issue(t, carry):
            for k in range(TOP_K):
                pltpu.make_async_copy(outs_ref.at[pl.ds(p_ref[0, 0, TOP_K * t + k], 1)],
                                      buf_ref.at[dst_slot, k, pl.ds(t, 1)], sem.at[dst_slot]).start()
            return carry

        lax.fori_loop(0, tc, issue, 0, unroll=2)

    @pl.when(i == 0)
    def _():
        gather_rows(pos_ref, 0)

    @pl.when(i + 1 < pl.num_programs(0))
    def _():
        gather_rows(pos_next_ref, 1 - slot)

    for k in range(TOP_K):
        pltpu.make_async_copy(outs_ref.at[pl.ds(0, tc)], buf_ref.at[slot, k], sem.at[slot]).wait()
    gates = rg_ref[...]
    ylo = yhi = None
    for k in range(TOP_K):
        lo, hi = _unpack_rows(buf_ref[slot, k])
        gk = gates[:, k:k + 1]
        ylo = gk * lo if ylo is None else ylo + gk * lo
        yhi = gk * hi if yhi is None else yhi + gk * hi
    y = jnp.concatenate([ylo, yhi], axis=1)
    _ln_epilogue(y, xres_ref, g_ref, b_ref, alpha, (of_ref, ob_ref, None))


def _combine(outs, pos, rg, xres, g, b, alpha, tc=128):
    t, d = xres.shape
    tc = min(tc, t)
    dh = d // 2
    nsteps = t // tc
    pos3 = pos.reshape(nsteps, 1, tc * TOP_K)
    return pl.pallas_call(
        functools.partial(_combine_body, alpha=alpha),
        grid=(nsteps,),
        in_specs=[
            pl.BlockSpec((1, 1, tc * TOP_K), lambda i: (i, 0, 0), memory_space=pltpu.SMEM),
            pl.BlockSpec((1, 1, tc * TOP_K), lambda i: (jnp.minimum(i + 1, nsteps - 1), 0, 0), memory_space=pltpu.SMEM),
            pl.BlockSpec((tc, LANES), lambda i: (i, 0)),
            pl.BlockSpec(memory_space=pl.ANY),
            pl.BlockSpec((tc, d), lambda i: (i, 0)),
            pl.BlockSpec((1, d), lambda i: (0, 0)),
            pl.BlockSpec((1, d), lambda i: (0, 0)),
        ],
        out_specs=[pl.BlockSpec((tc, d), lambda i: (i, 0)), pl.BlockSpec((tc, d), lambda i: (i, 0))],
        out_shape=[jax.ShapeDtypeStruct((t, d), F32), jax.ShapeDtypeStruct((t, d), BF16)],
        scratch_shapes=[pltpu.VMEM((2, TOP_K, tc, dh), I32), pltpu.SemaphoreType.DMA((2,))],
        compiler_params=_params("arbitrary"),
        name="moe_combine_ln",
    )(pos3, pos3, rg, outs, xres, g, b)


def _moe_layer(xf, xpk, router_w, router_b, w_gu, b_gu, w_dn, b_dn, layer, g, b, alpha, tm, tf, sub):
    t, d = xf.shape
    e = router_w.shape[1]
    w_pad = jnp.zeros((d, LANES), F32).at[:, :e].set(router_w)
    b_pad = jnp.full((1, LANES), -1e30, F32).at[0, :e].set(router_b)
    ri, rg, cnt = _router(xf, w_pad, b_pad)
    counts = cnt[0, :e].astype(I32)
    nt_e = (counts + tm - 1) // tm
    per_e = ((counts + jnp.maximum(nt_e, 1) * sub - 1) // (jnp.maximum(nt_e, 1) * sub)) * sub
    per_e = jnp.maximum(per_e, sub)
    tile_end = jnp.cumsum(nt_e)
    tile_start = tile_end - nt_e
    n_used = tile_end[-1]
    nt = (t * TOP_K) // tm + e
    e_idx = ri[:, 0:TOP_K]
    rank = ri[:, TOP_K:2 * TOP_K]
    per_a = per_e[e_idx]
    tile_k = jnp.floor((rank.astype(F32) + 0.5) / per_a.astype(F32)).astype(I32)
    pos = (tile_start[e_idx] + tile_k) * tm + rank - tile_k * per_a
    tid = jnp.minimum(jnp.arange(nt, dtype=I32), n_used - 1)
    tile_e = jnp.minimum(jnp.searchsorted(tile_end, tid, side="right"), e - 1).astype(I32)
    tile_rows = jnp.clip(counts[tile_e] - (tid - tile_start[tile_e]) * per_e[tile_e], 0, per_e[tile_e]).astype(I32)
    xs = _dispatch(xpk, pos, nt * tm)
    outs = _moe_experts(xs, w_gu, b_gu, w_dn, b_dn, layer, tile_e, tile_rows, n_used.reshape(1).astype(I32), tm, tf,
                        sub)
    return _combine(outs, pos, rg, xf, g, b, alpha)


def _xattn_layer(xf, xb, mem_b, wq, wkv, wo, g, b, alpha, batch):
    q = _matmul(xb, wq.astype(BF16), BF16)
    kv = _matmul(mem_b, wkv.astype(BF16), BF16)
    return _xattn(q, kv, wo.astype(BF16), xf, g, b, alpha, batch)


def _even_mixer(xf, xb, w_in, conv_w, conv_b, a_log, dt_bias, d_skip, ssd_norm, gk_w2, gk_b, gla_norm, w_out, g, b,
                alpha, batch):
    d = xf.shape[1]
    grp = SSD_GROUPS
    heads = a_log.shape[1]
    r_heads = heads // grp
    d_ssd = heads * SSD_HEAD_DIM
    n_state = (conv_w.shape[1] - d_ssd) // (2 * grp)
    conv_ch = d_ssd + 2 * grp * n_state
    rank = gk_w2.shape[1]
    dk = gk_w2.shape[2] // GLA_HEADS
    dv = gla_norm.shape[0]
    o_xbc = d_ssd
    o_dt = o_xbc + conv_ch
    o_q = o_dt + 2 * heads
    o_k = o_q + GLA_HEADS * dk
    o_v = o_k + GLA_HEADS * dk
    o_g = o_v + GLA_HEADS * dv
    o_lr = o_g + GLA_HEADS * dv
    assert 2 * r_heads <= LANES and 2 * rank <= LANES and o_lr + 2 * rank == w_in.shape[1]
    w_big = jnp.concatenate([w_in[:, :o_dt], w_in[:, o_q:o_lr]], axis=1).astype(BF16)
    w_dt = w_in[:, o_dt:o_q].reshape(d, 2, grp, r_heads).transpose(0, 2, 1, 3).reshape(d, grp, 2 * r_heads)
    w_dt = jnp.pad(w_dt, ((0, 0), (0, 0), (0, LANES - 2 * r_heads))).reshape(d, grp * LANES)
    w_lr = jnp.pad(w_in[:, o_lr:], ((0, 0), (0, LANES - 2 * rank)))
    w_small = jnp.concatenate([w_dt, w_lr], axis=1).astype(BF16)
    proj = _matmul(xb, w_big, BF16)
    psmall = _matmul(xb, w_small, F32)

    xbc = _dwconv(proj, o_xbc, conv_ch, conv_w, conv_b, batch, act=True)

    def group_lanes(p):
        p = p.reshape(2, grp, r_heads).transpose(1, 0, 2).reshape(grp, 1, 2 * r_heads)
        return jnp.pad(p, ((0, 0), (0, 0), (0, LANES - 2 * r_heads)))

    y = _ssd(xbc, proj, psmall, group_lanes(a_log), group_lanes(dt_bias),
             jnp.repeat(d_skip, SSD_HEAD_DIM).reshape(1, d_ssd), ssd_norm.reshape(1, d_ssd), batch, d_ssd, n_state)

    w2 = gk_w2.reshape(2, rank, GLA_HEADS, dk).transpose(2, 0, 1, 3)
    w2pad = jnp.zeros((GLA_HEADS, 2, LANES, dk), F32)
    for dirn in range(2):
        w2pad = w2pad.at[:, dirn, dirn * rank:(dirn + 1) * rank, :].set(w2[:, dirn])
    w2pad = w2pad.reshape(GLA_HEADS, 2 * LANES, dk)
    gkb = gk_b.reshape(2, GLA_HEADS, dk).transpose(1, 0, 2)
    q0 = o_dt
    k0 = q0 + GLA_HEADS * dk
    v0 = k0 + GLA_HEADS * dk
    g0 = v0 + GLA_HEADS * dv
    o = _gla(proj, psmall, w2pad, gkb, gla_norm.reshape(1, dv), batch, q0, k0, v0, g0, grp, dk, dv)
    return _even_out(y, o, w_out.astype(BF16), xf, g, b, alpha)


def _odd_mixer(xf, xb, w_in, conv_w, conv_b, wa, ba, wx, bx, lam, w_out, g, b, alpha, batch):
    width = conv_w.shape[1]
    proj = _matmul(xb, w_in.astype(BF16), BF16)
    vc = _dwconv(proj, width, width, conv_w, conv_b, batch, act=False)
    hf, hb = _rglru(vc, wa.astype(BF16), wx.astype(BF16), ba, bx, lam, batch)
    return _odd_out(proj, hf, hb, w_out.astype(BF16), xf, g, b, alpha)


def kernel(x, mem, ln_g, ln_b, even_w_in, even_conv_w, even_conv_b, ssd_a_log, ssd_dt_bias, ssd_d, ssd_norm, gla_gk_w2, gla_gk_b, gla_norm, even_w_out, odd_w_in, odd_conv_w, odd_conv_b, rg_wa, rg_ba, rg_wx, rg_bx, rg_lam, odd_w_out, xattn_wq, xattn_wkv, xattn_wo, router_w, router_b, moe_w_gu, moe_b_gu, moe_w_dn, moe_b_dn):
    batch, seq, d = x.shape
    depth = ln_g.shape[0]
    alpha = float((2 * depth) ** 0.25)
    xf = x.reshape(batch * seq, d)
    xb = xf.astype(BF16)
    mem_b = mem.reshape(-1, d).astype(BF16)
    for layer in range(depth):
        i = layer // 2
        lg = lambda s: ln_g[layer, s].reshape(1, d)
        lb = lambda s: ln_b[layer, s].reshape(1, d)
        if layer % 2 == 0:
            xf, xb = _even_mixer(xf, xb, even_w_in[i], even_conv_w[i], even_conv_b[i], ssd_a_log[i], ssd_dt_bias[i],
                                 ssd_d[i], ssd_norm[i], gla_gk_w2[i], gla_gk_b[i], gla_norm[i], even_w_out[i],
                                 lg(0), lb(0), alpha, batch)
        else:
            xf, xb = _odd_mixer(xf, xb, odd_w_in[i], odd_conv_w[i], odd_conv_b[i], rg_wa[i], rg_ba[i], rg_wx[i],
                                rg_bx[i], rg_lam[i], odd_w_out[i], lg(0), lb(0), alpha, batch)
        xf, xpk = _xattn_layer(xf, xb, mem_b, xattn_wq[layer], xattn_wkv[layer], xattn_wo[layer], lg(1), lb(1), alpha,
                               batch)
        xf, xb = _moe_layer(xf, xpk, router_w[layer], router_b[layer], moe_w_gu, moe_b_gu, moe_w_dn, moe_b_dn, layer,
                            lg(2), lb(2), alpha, MOE_TILE, MOE_FF_TILE, MOE_SUB)
    return xf.reshape(batch, seq, d)
```

```python
import functools

import jax
import jax.numpy as jnp
from jax import lax
from jax.experimental import pallas as pl
from jax.experimental.pallas import tpu as pltpu

F32 = jnp.float32
BF16 = jnp.bfloat16
I32 = jnp.int32

SSD_HEAD_DIM = 64
SSD_GROUPS = 4
SSD_CHUNK = 128
GLA_HEADS = 4
GLA_NORMALIZER = 16.0
GLA_CHUNK = 64
GLA_GROUP = 4
RG_C = 8.0
XATTN_HEADS = 4
TOP_K = 4
SWIGLU_ALPHA = 1.702
SWIGLU_LIMIT = 7.0
LN_EPS = 1e-5
RMS_EPS = 1e-6

LANES = 128
VMEM_LIMIT_BYTES = 56 * 1024 * 1024
CONV_HALO = 16
MOE_TILE = 1024
MOE_FF_TILE = 512
MOE_SUB = 256
NEG_BIG = -3.0e38

TN_DIMS = (((0,), (0,)), ((), ()))
NT_DIMS = (((1,), (1,)), ((), ()))


def _params(*sem):
    return pltpu.CompilerParams(dimension_semantics=sem, vmem_limit_bytes=VMEM_LIMIT_BYTES)


def _dot(a, b):
    return jnp.dot(a, b, preferred_element_type=F32)


def _dot_nt(a, b):
    return lax.dot_general(a, b, NT_DIMS, preferred_element_type=F32)


def _dot_tn(a, b):
    return lax.dot_general(a, b, TN_DIMS, preferred_element_type=F32)


def _sigmoid(x):
    return 1.0 / (1.0 + jnp.exp(-x))


def _softplus(x):
    return jnp.maximum(x, 0.0) + jnp.log(1.0 + jnp.exp(-jnp.abs(x)))


def _split3(x):
    hi = x.astype(BF16)
    rem = x - hi.astype(F32)
    mid = rem.astype(BF16)
    lo = (rem - mid.astype(F32)).astype(BF16)
    return hi, mid, lo


def _tri_cumsum(tri, x):
    hi, mid, lo = _split3(x)
    return _dot(tri, hi) + _dot(tri, mid) + _dot(tri, lo)


def _tri_cumsum_tn(x, tri_t):
    hi, mid, lo = _split3(x)
    return _dot_tn(hi, tri_t) + _dot_tn(mid, tri_t) + _dot_tn(lo, tri_t)


def _layer_norm(v, g, b):
    mu = jnp.mean(v, axis=-1, keepdims=True)
    c = v - mu
    var = jnp.mean(c * c, axis=-1, keepdims=True)
    return c * lax.rsqrt(var + LN_EPS) * g + b


def _pack_rows(v):
    h = v.shape[1] // 2
    lo = pltpu.bitcast(v[:, :h].astype(BF16).astype(F32), I32)
    hi = pltpu.bitcast(v[:, h:].astype(BF16).astype(F32), I32)
    return (lo & jnp.int32(-65536)) | (lax.shift_right_logical(hi, jnp.int32(16)))


def _unpack_rows(w):
    lo = pltpu.bitcast(w & jnp.int32(-65536), F32)
    hi = pltpu.bitcast(lax.shift_left(w, jnp.int32(16)), F32)
    return lo, hi


def _largest_tile(n, cap, *also):
    for c in range(min(cap, n) // LANES * LANES, 0, -LANES):
        if n % c == 0 and all(a % c == 0 for a in also):
            return c
    return n


def _mm_body(a_ref, w_ref, o_ref):
    o_ref[...] = _dot(a_ref[...], w_ref[...]).astype(o_ref.dtype)


def _matmul(a, w, out_dtype, tm=1024, tn=1024):
    m, k = a.shape
    n = w.shape[1]
    tm = min(tm, m)
    tn = _largest_tile(n, tn)
    assert m % tm == 0
    return pl.pallas_call(
        _mm_body,
        grid=(n // tn, m // tm),
        in_specs=[pl.BlockSpec((tm, k), lambda j, i: (i, 0)), pl.BlockSpec((k, tn), lambda j, i: (0, j))],
        out_specs=pl.BlockSpec((tm, tn), lambda j, i: (i, j)),
        out_shape=jax.ShapeDtypeStruct((m, n), out_dtype),
        compiler_params=_params("parallel", "parallel"),
        name="proj_matmul",
    )(a, w)


def _conv_body(x_ref, w_ref, b_ref, o_ref, *, act, rows):
    seq = x_ref.shape[0]
    nchunks = seq // rows
    tot = rows + 2 * CONV_HALO
    w = w_ref[...]
    b = b_ref[...]

    def body(i, carry):
        r0 = pl.multiple_of(i * rows, rows)
        main = x_ref[pl.ds(r0, rows), :].astype(F32)
        p0 = pl.multiple_of(jnp.maximum(r0 - CONV_HALO, 0), CONV_HALO)
        prev = x_ref[pl.ds(p0, CONV_HALO), :].astype(F32)
        prev = jnp.where(i > 0, prev, 0.0)
        n0 = pl.multiple_of(jnp.minimum(r0 + rows, seq - CONV_HALO), CONV_HALO)
        nxt = x_ref[pl.ds(n0, CONV_HALO), :].astype(F32)
        nxt = jnp.where(i < nchunks - 1, nxt, 0.0)
        ext = jnp.concatenate([prev, main, nxt], axis=0)
        acc = b + w[1:2] * main
        for j in (0, 2, 3):
            shift = (1 - j) % tot
            acc = acc + w[j:j + 1] * pltpu.roll(ext, shift, 0)[CONV_HALO:CONV_HALO + rows]
        if act:
            acc = acc * _sigmoid(acc)
        o_ref[pl.ds(r0, rows), :] = acc.astype(o_ref.dtype)
        return carry

    lax.fori_loop(0, nchunks, body, 0)


def _dwconv(x, col0, width, conv_w, conv_b, batch, act, cw=512):
    t = x.shape[0]
    seq = t // batch
    cw = _largest_tile(width, cw, col0)
    assert width % cw == 0 and col0 % cw == 0
    rows = min(256, seq)
    cb0 = col0 // cw
    return pl.pallas_call(
        functools.partial(_conv_body, act=act, rows=rows),
        grid=(batch, width // cw),
        in_specs=[
            pl.BlockSpec((seq, cw), lambda b, j: (b, cb0 + j)),
            pl.BlockSpec((conv_w.shape[0], cw), lambda b, j: (0, j)),
            pl.BlockSpec((1, cw), lambda b, j: (0, j)),
        ],
        out_specs=pl.BlockSpec((seq, cw), lambda b, j: (b, j)),
        out_shape=jax.ShapeDtypeStruct((t, width), BF16),
        compiler_params=_params("parallel", "parallel"),
        name="dwconv",
    )(x, conv_w, conv_b.reshape(1, -1))


def _ssd_body(xs_ref, b_ref, c_ref, z_ref, ps_ref, alog_ref, dtb_ref, dsk_ref, nrm_ref, y_ref, yacc_ref, h_ref):
    seq, hp = xs_ref.shape
    q = SSD_CHUNK
    nc = seq // q
    p = SSD_HEAD_DIM
    r_heads = hp // p
    a_neg = -jnp.exp(alog_ref[0])
    dtb = dtb_ref[0]
    row = lax.broadcasted_iota(I32, (q, q), 0)
    col = lax.broadcasted_iota(I32, (q, q), 1)
    lower = (col <= row).astype(BF16)
    upper = (col >= row).astype(BF16)
    e_row = lax.broadcasted_iota(I32, (LANES, hp), 0)
    e_col = lax.broadcasted_iota(I32, (LANES, hp), 1) // p
    pw = min(LANES, hp)
    piece_head = lax.broadcasted_iota(I32, (q, pw), 1) // p

    h_ref[...] = jnp.zeros_like(h_ref)

    def chunk_dir(c, dirn, final):
        tri, tri_t = (lower, upper) if dirn == 0 else (upper, lower)
        mask = (col <= row) if dirn == 0 else (col >= row)
        expand = (e_row == e_col + dirn * r_heads).astype(BF16)
        edge = q - 1 if dirn == 0 else 0
        r0 = pl.multiple_of(c * q, q)
        x = xs_ref[pl.ds(r0, q), :]
        xf = x.astype(F32)
        bm = b_ref[pl.ds(r0, q), :]
        cm = c_ref[pl.ds(r0, q), :]
        dt = _softplus(ps_ref[pl.ds(r0, q), :] + dtb)
        dta = dt * a_neg
        acum = _tri_cumsum(tri, dta)
        acum_t = _tri_cumsum_tn(dta, tri_t)
        alast = acum[edge:edge + 1, :]
        stack = jnp.concatenate([dt, jnp.exp(acum), dt * jnp.exp(alast - acum)], axis=0).astype(BF16)
        ex = _dot(stack, expand)
        xdt = (xf * ex[0:q]).astype(BF16)
        eac = ex[q:2 * q]
        xw = (xf * ex[2 * q:3 * q]).astype(BF16)
        cb = _dot_nt(cm, bm)
        yd = []
        for j in range(hp // pw):
            xpiece = xdt[:, j * pw:(j + 1) * pw]
            ypiece = None
            for a in range(pw // p):
                cr = dirn * r_heads + j * (pw // p) + a
                seg = acum[:, cr:cr + 1] - acum_t[cr:cr + 1, :]
                dec = jnp.where(mask, jnp.exp(jnp.minimum(seg, 0.0)), 0.0)
                wgt = (cb * dec).astype(BF16)
                xa = xpiece if pw == p else jnp.where(piece_head == a, xpiece, jnp.zeros_like(xpiece))
                ya = _dot(wgt, xa)
                ypiece = ya if ypiece is None else ypiece + ya
            yd.append(ypiece)
        y = jnp.concatenate(yd, axis=1) if len(yd) > 1 else yd[0]
        hst = h_ref[dirn]
        y = y + _dot(cm, hst.astype(BF16)) * eac
        ea = jnp.broadcast_to(jnp.exp(alast), (8, LANES))
        ea_hi = ea.astype(BF16)
        ea_lo = (ea - ea_hi.astype(F32)).astype(BF16)
        hdec = (_dot(ea_hi, expand) + _dot(ea_lo, expand))[0:1]
        h_ref[dirn] = hst * hdec + _dot_tn(bm, xw)
        if not final:
            yacc_ref[pl.ds(r0, q), :] = y
        else:
            yt = yacc_ref[pl.ds(r0, q), :] + y + dsk_ref[...] * xf
            zf = z_ref[pl.ds(r0, q), :].astype(F32)
            yt = yt * (zf * _sigmoid(zf))
            ms = jnp.mean(yt * yt, axis=-1, keepdims=True)
            y_ref[pl.ds(r0, q), :] = (yt * lax.rsqrt(ms + RMS_EPS) * nrm_ref[...]).astype(y_ref.dtype)

    def both_dirs(final):
        def body(i, carry):
            chunk_dir(i, 0, final)
            chunk_dir(nc - 1 - i, 1, final)
            return carry
        return body

    assert nc % 2 == 0
    lax.fori_loop(0, nc // 2, both_dirs(False), 0)
    lax.fori_loop(nc // 2, nc, both_dirs(True), 0)


def _ssd(xbc, proj, psmall, alog, dtb, dskip, nrm, batch, d_ssd, n_state):
    t = xbc.shape[0]
    seq = t // batch
    g = SSD_GROUPS
    hp = d_ssd // g
    nb0 = d_ssd // n_state
    return pl.pallas_call(
        _ssd_body,
        grid=(batch, g),
        in_specs=[
            pl.BlockSpec((seq, hp), lambda b, j: (b, j)),
            pl.BlockSpec((seq, n_state), lambda b, j: (b, nb0 + j)),
            pl.BlockSpec((seq, n_state), lambda b, j: (b, nb0 + g + j)),
            pl.BlockSpec((seq, hp), lambda b, j: (b, j)),
            pl.BlockSpec((seq, LANES), lambda b, j: (b, j)),
            pl.BlockSpec((1, 1, LANES), lambda b, j: (j, 0, 0)),
            pl.BlockSpec((1, 1, LANES), lambda b, j: (j, 0, 0)),
            pl.BlockSpec((1, hp), lambda b, j: (0, j)),
            pl.BlockSpec((1, hp), lambda b, j: (0, j)),
        ],
        out_specs=pl.BlockSpec((seq, hp), lambda b, j: (b, j)),
        out_shape=jax.ShapeDtypeStruct((t, d_ssd), BF16),
        scratch_shapes=[pltpu.VMEM((seq, hp), F32), pltpu.VMEM((2, n_state, hp), F32)],
        compiler_params=_params("parallel", "parallel"),
        name="ssd_scan",
    )(xbc, xbc, xbc, proj, psmall, alog, dtb, dskip, nrm)


def _gla_body(q_ref, k_ref, v_ref, go_ref, ps_ref, w2_ref, gkb_ref, nrm_ref, o_ref, oacc_ref, s_ref):
    seq, dk = q_ref.shape
    dv = v_ref.shape[1]
    cq = GLA_CHUNK
    gq = GLA_GROUP * cq
    ng = seq // gq
    scale = dk ** -0.5
    row = lax.broadcasted_iota(I32, (gq, gq), 0)
    col = lax.broadcasted_iota(I32, (gq, gq), 1)
    same_chunk = (row // cq) == (col // cq)
    masks = (same_chunk & (col <= row), same_chunk & (col >= row))
    tris = tuple(m.astype(BF16) for m in masks)

    s_ref[...] = jnp.zeros_like(s_ref)

    def group_dir(gi, dirn, final):
        w2 = w2_ref[0, dirn * LANES:(dirn + 1) * LANES, :].astype(BF16)
        gb = gkb_ref[0, dirn:dirn + 1, :]
        edge = cq - 1 if dirn == 0 else 0
        r0 = pl.multiple_of(gi * gq, gq)
        lr = ps_ref[pl.ds(r0, gq), :].astype(BF16)
        gl = _dot(lr, w2) + gb
        g = -_softplus(-gl) * (1.0 / GLA_NORMALIZER)
        bcum = _tri_cumsum(tris[dirn], g)
        qf = q_ref[pl.ds(r0, gq), :].astype(F32) * scale
        kf = k_ref[pl.ds(r0, gq), :].astype(F32)
        v = v_ref[pl.ds(r0, gq), :]
        q_in = (qf * jnp.exp(bcum)).astype(BF16)
        k_in = (kf * jnp.exp(-bcum)).astype(BF16)
        att = jnp.where(masks[dirn], _dot_nt(q_in, k_in), 0.0).astype(BF16)
        o_intra = _dot(att, v)
        order = range(GLA_GROUP) if dirn == 0 else range(GLA_GROUP - 1, -1, -1)
        for j in order:
            rows = slice(j * cq, (j + 1) * cq)
            blast = bcum[j * cq + edge:j * cq + edge + 1, :]
            k_st = (kf[rows] * jnp.exp(blast - bcum[rows])).astype(BF16)
            st = s_ref[dirn]
            o = o_intra[rows] + _dot_nt(q_in[rows], st.astype(BF16))
            s_ref[dirn] = st * jnp.exp(blast) + _dot_tn(v[rows], k_st)
            rj = pl.multiple_of(r0 + j * cq, cq)
            if not final:
                oacc_ref[pl.ds(rj, cq), :] = o
            else:
                ot = oacc_ref[pl.ds(rj, cq), :] + o
                ms = jnp.mean(ot * ot, axis=-1, keepdims=True)
                gf = go_ref[pl.ds(rj, cq), :].astype(F32)
                o_ref[pl.ds(rj, cq), :] = (ot * lax.rsqrt(ms + RMS_EPS) * nrm_ref[...] * (gf * _sigmoid(gf))).astype(o_ref.dtype)

    def both_dirs(final):
        def body(i, carry):
            group_dir(i, 0, final)
            group_dir(ng - 1 - i, 1, final)
            return carry
        return body

    assert ng % 2 == 0
    lax.fori_loop(0, ng // 2, both_dirs(False), 0)
    lax.fori_loop(ng // 2, ng, both_dirs(True), 0)


def _gla(proj, psmall, w2pad, gkb, nrm, batch, q0, k0, v0, g0, gk_blk, dk, dv):
    t = proj.shape[0]
    seq = t // batch
    h = GLA_HEADS
    return pl.pallas_call(
        _gla_body,
        grid=(batch, h),
        in_specs=[
            pl.BlockSpec((seq, dk), lambda b, j: (b, q0 // dk + j)),
            pl.BlockSpec((seq, dk), lambda b, j: (b, k0 // dk + j)),
            pl.BlockSpec((seq, dv), lambda b, j: (b, v0 // dv + j)),
            pl.BlockSpec((seq, dv), lambda b, j: (b, g0 // dv + j)),
            pl.BlockSpec((seq, LANES), lambda b, j: (b, gk_blk)),
            pl.BlockSpec((1, 2 * LANES, dk), lambda b, j: (j, 0, 0)),
            pl.BlockSpec((1, 2, dk), lambda b, j: (j, 0, 0)),
            pl.BlockSpec((1, dv), lambda b, j: (0, 0)),
        ],
        out_specs=pl.BlockSpec((seq, dv), lambda b, j: (b, j)),
        out_shape=jax.ShapeDtypeStruct((t, h * dv), BF16),
        scratch_shapes=[pltpu.VMEM((seq, dv), F32), pltpu.VMEM((2, dv, dk), F32)],
        compiler_params=_params("parallel", "parallel"),
        name="gla_scan",
    )(proj, proj, proj, proj, psmall, w2pad, gkb, nrm)


def _ln_epilogue(h, xres_ref, g_ref, b_ref, alpha, outs):
    v = _layer_norm(alpha * xres_ref[...] + h, g_ref[...], b_ref[...])
    of_ref, ob_ref, opk_ref = outs
    of_ref[...] = v
    if ob_ref is not None:
        ob_ref[...] = v.astype(BF16)
    if opk_ref is not None:
        opk_ref[...] = _pack_rows(v)


def _even_out_body(y_ref, o_ref, w_ref, xres_ref, g_ref, b_ref, of_ref, ob_ref, *, alpha):
    k1 = y_ref.shape[1]
    h = _dot(y_ref[...], w_ref[0:k1, :]) + _dot(o_ref[...], w_ref[k1:, :])
    _ln_epilogue(h, xres_ref, g_ref, b_ref, alpha, (of_ref, ob_ref, None))


def _even_out(y, o, w, xres, g, b, alpha, tm=512):
    t, k1 = y.shape
    k2 = o.shape[1]
    d = w.shape[1]
    tm = min(tm, t)
    assert t % tm == 0
    return pl.pallas_call(
        functools.partial(_even_out_body, alpha=alpha),
        grid=(t // tm,),
        in_specs=[
            pl.BlockSpec((tm, k1), lambda i: (i, 0)),
            pl.BlockSpec((tm, k2), lambda i: (i, 0)),
            pl.BlockSpec((k1 + k2, d), lambda i: (0, 0), pipeline_mode=pl.Buffered(1)),
            pl.BlockSpec((tm, d), lambda i: (i, 0)),
            pl.BlockSpec((1, d), lambda i: (0, 0)),
            pl.BlockSpec((1, d), lambda i: (0, 0)),
        ],
        out_specs=[pl.BlockSpec((tm, d), lambda i: (i, 0)), pl.BlockSpec((tm, d), lambda i: (i, 0))],
        out_shape=[jax.ShapeDtypeStruct((t, d), F32), jax.ShapeDtypeStruct((t, d), BF16)],
        compiler_params=_params("parallel"),
        name="even_out_ln",
    )(y, o, w, xres, g, b)


def _gelu_tanh(x):
    return 0.5 * x * (1.0 + jnp.tanh(0.7978845608028654 * (x + 0.044715 * x * x * x)))


def _odd_out_body(gate_ref, hf_ref, hb_ref, w_ref, xres_ref, g_ref, b_ref, of_ref, ob_ref, *, alpha):
    hsum = hf_ref[...].astype(F32) + hb_ref[...].astype(F32)
    a = (_gelu_tanh(gate_ref[...].astype(F32)) * hsum).astype(BF16)
    _ln_epilogue(_dot(a, w_ref[...]), xres_ref, g_ref, b_ref, alpha, (of_ref, ob_ref, None))


def _odd_out(proj, hf, hb, w, xres, g, b, alpha, tm=512):
    t, kw = hf.shape
    d = w.shape[1]
    tm = min(tm, t)
    return pl.pallas_call(
        functools.partial(_odd_out_body, alpha=alpha),
        grid=(t // tm,),
        in_specs=[
            pl.BlockSpec((tm, kw), lambda i: (i, 0)),
            pl.BlockSpec((tm, kw), lambda i: (i, 0)),
            pl.BlockSpec((tm, kw), lambda i: (i, 0)),
            pl.BlockSpec((kw, d), lambda i: (0, 0), pipeline_mode=pl.Buffered(1)),
            pl.BlockSpec((tm, d), lambda i: (i, 0)),
            pl.BlockSpec((1, d), lambda i: (0, 0)),
            pl.BlockSpec((1, d), lambda i: (0, 0)),
        ],
        out_specs=[pl.BlockSpec((tm, d), lambda i: (i, 0)), pl.BlockSpec((tm, d), lambda i: (i, 0))],
        out_shape=[jax.ShapeDtypeStruct((t, d), F32), jax.ShapeDtypeStruct((t, d), BF16)],
        compiler_params=_params("parallel"),
        name="odd_out_ln",
    )(proj, hf, hb, w, xres, g, b)


def _rglru_body(vf_ref, vb_ref, wa_ref, wx_ref, ba_ref, bx_ref, lam_ref, hf_ref, hb_ref, af_ref, uf_ref, ab_ref, ub_ref, hc_ref):
    tt, width = vf_ref.shape
    nb, bw = wa_ref.shape[1], wa_ref.shape[2]

    @pl.when(pl.program_id(1) == 0)
    def _():
        hc_ref[...] = jnp.zeros_like(hc_ref)

    sp = _softplus(-lam_ref[...])
    for d, (v_ref, a_s, u_s) in enumerate(((vf_ref, af_ref, uf_ref), (vb_ref, ab_ref, ub_ref))):
        for n in range(nb):
            cs = slice(n * bw, (n + 1) * bw)
            vb16 = v_ref[:, cs]
            r = _sigmoid(_dot(vb16, wa_ref[d, n]) + ba_ref[d:d + 1, cs])
            ig = _sigmoid(_dot(vb16, wx_ref[d, n]) + bx_ref[d:d + 1, cs])
            a = jnp.exp(-RG_C * r * sp[d:d + 1, cs])
            a_s[:, cs] = a
            u_s[:, cs] = jnp.sqrt(1.0 - a * a) * (ig * vb16.astype(F32))

    def step(t8, carry):
        hf, hb = carry
        for s in range(8):
            tf = t8 * 8 + s
            tb = tt - 1 - tf
            hf = af_ref[pl.ds(tf, 1), :] * hf + uf_ref[pl.ds(tf, 1), :]
            uf_ref[pl.ds(tf, 1), :] = hf
            hb = ab_ref[pl.ds(tb, 1), :] * hb + ub_ref[pl.ds(tb, 1), :]
            ub_ref[pl.ds(tb, 1), :] = hb
        return hf, hb

    hf, hb = lax.fori_loop(0, tt // 8, step, (hc_ref[0:1, :], hc_ref[1:2, :]))
    hc_ref[0:1, :] = hf
    hc_ref[1:2, :] = hb
    hf_ref[...] = uf_ref[...].astype(hf_ref.dtype)
    hb_ref[...] = ub_ref[...].astype(hb_ref.dtype)


def _rglru(vc, wa, wx, ba, bx, lam, batch, tt=256):
    t, width = vc.shape
    seq = t // batch
    tt = min(tt, seq)
    ntt = seq // tt
    nb, bw = wa.shape[1], wa.shape[2]
    full = lambda *shape: pl.BlockSpec(shape, lambda b, j: (0,) * len(shape))
    return pl.pallas_call(
        _rglru_body,
        grid=(batch, ntt),
        in_specs=[
            pl.BlockSpec((tt, width), lambda b, j: (b * ntt + j, 0)),
            pl.BlockSpec((tt, width), lambda b, j: (b * ntt + ntt - 1 - j, 0)),
            full(2, nb, bw, bw),
            full(2, nb, bw, bw),
            full(2, width),
            full(2, width),
            full(2, width),
        ],
        out_specs=[
            pl.BlockSpec((tt, width), lambda b, j: (b * ntt + j, 0)),
            pl.BlockSpec((tt, width), lambda b, j: (b * ntt + ntt - 1 - j, 0)),
        ],
        out_shape=[jax.ShapeDtypeStruct((t, width), BF16), jax.ShapeDtypeStruct((t, width), BF16)],
        scratch_shapes=[pltpu.VMEM((tt, width), F32) for _ in range(4)] + [pltpu.VMEM((8, width), F32)],
        compiler_params=_params("parallel", "arbitrary"),
        name="rglru",
    )(vc, vc, wa, wx, ba, bx, lam)


def _xattn_body(q_ref, k_ref, v_ref, wo_ref, xres_ref, g_ref, b_ref, of_ref, opk_ref, *, alpha):
    tm, d = q_ref.shape
    hd = d // XATTN_HEADS
    scale = hd ** -0.5
    heads = []
    for h in range(XATTN_HEADS):
        cs = slice(h * hd, (h + 1) * hd)
        s = _dot_nt(q_ref[:, cs], k_ref[:, cs]) * scale
        s = s - jnp.max(s, axis=-1, keepdims=True)
        e = jnp.exp(s)
        pr = (e / jnp.sum(e, axis=-1, keepdims=True)).astype(BF16)
        heads.append(_dot(pr, v_ref[:, cs]).astype(BF16))
    attn = jnp.concatenate(heads, axis=1)
    _ln_epilogue(_dot(attn, wo_ref[...]), xres_ref, g_ref, b_ref, alpha, (of_ref, None, opk_ref))


def _xattn(q, kv, wo, xres, g, b, alpha, batch, tm=512):
    t, d = q.shape
    seq = t // batch
    mem = kv.shape[0] // batch
    tm = min(tm, seq)
    return pl.pallas_call(
        functools.partial(_xattn_body, alpha=alpha),
        grid=(t // tm,),
        in_specs=[
            pl.BlockSpec((tm, d), lambda i: (i, 0)),
            pl.BlockSpec((mem, d), lambda i: (i * tm // seq, 0)),
            pl.BlockSpec((mem, d), lambda i: (i * tm // seq, 1)),
            pl.BlockSpec((d, d), lambda i: (0, 0)),
            pl.BlockSpec((tm, d), lambda i: (i, 0)),
            pl.BlockSpec((1, d), lambda i: (0, 0)),
            pl.BlockSpec((1, d), lambda i: (0, 0)),
        ],
        out_specs=[pl.BlockSpec((tm, d), lambda i: (i, 0)), pl.BlockSpec((tm, d // 2), lambda i: (i, 0))],
        out_shape=[jax.ShapeDtypeStruct((t, d), F32), jax.ShapeDtypeStruct((t, d // 2), I32)],
        compiler_params=_params("parallel"),
        name="xattn_out_ln",
    )(q, kv, kv, wo, xres, g, b)


def _router_body(x_ref, wh_ref, wl_ref, b_ref, ri_ref, rg_ref, cnt_ref, carry_ref):
    tr = x_ref.shape[0]

    @pl.when(pl.program_id(0) == 0)
    def _():
        carry_ref[...] = jnp.zeros_like(carry_ref)

    x = x_ref[...]
    xh = x.astype(BF16)
    xl = (x - xh.astype(F32)).astype(BF16)
    logits = _dot(xh, wh_ref[...]) + (_dot(xh, wl_ref[...]) + _dot(xl, wh_ref[...])) + b_ref[...]
    lane = lax.broadcasted_iota(I32, (tr, LANES), 1).astype(F32)
    work = logits
    sels, vals, idxs = [], [], []
    for _ in range(TOP_K):
        m = jnp.max(work, axis=-1, keepdims=True)
        idx = jnp.min(jnp.where(work == m, lane, float(LANES)), axis=-1, keepdims=True)
        sel = lane == idx
        sels.append(sel)
        vals.append(m)
        idxs.append(idx)
        work = jnp.where(sel, NEG_BIG, work)
    exps = [jnp.exp(v - vals[0]) for v in vals]
    den = exps[0]
    for e in exps[1:]:
        den = den + e
    onehot = jnp.zeros((tr, LANES), F32)
    for sel in sels:
        onehot = onehot + sel.astype(F32)
    row = lax.broadcasted_iota(I32, (tr, tr), 0)
    col = lax.broadcasted_iota(I32, (tr, tr), 1)
    strict = (col < row).astype(BF16)
    before = _dot(strict, onehot.astype(BF16)) + carry_ref[...]
    ri = jnp.zeros((tr, LANES), F32)
    rg = jnp.zeros((tr, LANES), F32)
    for k in range(TOP_K):
        rank = jnp.sum(jnp.where(sels[k], before, 0.0), axis=-1, keepdims=True)
        ri = jnp.where(lane == float(k), idxs[k], ri)
        ri = jnp.where(lane == float(TOP_K + k), rank, ri)
        rg = jnp.where(lane == float(k), exps[k] / den, rg)
    ri_ref[...] = ri.astype(I32)
    rg_ref[...] = rg
    total = carry_ref[...] + jnp.sum(onehot, axis=0, keepdims=True)
    carry_ref[...] = total
    cnt_ref[...] = total


def _router(x, w_pad, b_pad, tr=512):
    t, d = x.shape
    tr = min(tr, t)
    w_hi = w_pad.astype(BF16)
    w_lo = (w_pad - w_hi.astype(F32)).astype(BF16)
    return pl.pallas_call(
        _router_body,
        grid=(t // tr,),
        in_specs=[
            pl.BlockSpec((tr, d), lambda i: (i, 0)),
            pl.BlockSpec((d, LANES), lambda i: (0, 0)),
            pl.BlockSpec((d, LANES), lambda i: (0, 0)),
            pl.BlockSpec((1, LANES), lambda i: (0, 0)),
        ],
        out_specs=[
            pl.BlockSpec((tr, LANES), lambda i: (i, 0)),
            pl.BlockSpec((tr, LANES), lambda i: (i, 0)),
            pl.BlockSpec((1, LANES), lambda i: (0, 0)),
        ],
        out_shape=[
            jax.ShapeDtypeStruct((t, LANES), I32),
            jax.ShapeDtypeStruct((t, LANES), F32),
            jax.ShapeDtypeStruct((1, LANES), F32),
        ],
        scratch_shapes=[pltpu.VMEM((1, LANES), F32)],
        compiler_params=_params("arbitrary"),
        name="moe_router",
    )(x, w_hi, w_lo, b_pad)


def _dispatch_body(pos_ref, xpk_ref, xs_in_ref, xs_ref, sem):
    del xs_in_ref
    ts = xpk_ref.shape[0]

    def issue(t, carry):
        for k in range(TOP_K):
            pltpu.make_async_copy(xpk_ref.at[pl.ds(t, 1)], xs_ref.at[pl.ds(pos_ref[0, 0, TOP_K * t + k], 1)], sem).start()
        return carry

    lax.fori_loop(0, ts, issue, 0, unroll=2)
    for k in range(TOP_K):
        pltpu.make_async_copy(xpk_ref, xs_ref.at[pl.ds(0, ts)], sem).wait()


def _dispatch(xpk, pos, n_rows, ts=256):
    t, dh = xpk.shape
    ts = min(ts, t)
    pos3 = pos.reshape(t // ts, 1, ts * TOP_K)
    xs0 = jnp.zeros((n_rows, dh), I32)
    return pl.pallas_call(
        _dispatch_body,
        grid=(t // ts,),
        in_specs=[
            pl.BlockSpec((1, 1, ts * TOP_K), lambda i: (i, 0, 0), memory_space=pltpu.SMEM),
            pl.BlockSpec((ts, dh), lambda i: (i, 0)),
            pl.BlockSpec(memory_space=pl.ANY),
        ],
        out_specs=pl.BlockSpec(memory_space=pl.ANY),
        out_shape=jax.ShapeDtypeStruct((n_rows, dh), I32),
        scratch_shapes=[pltpu.SemaphoreType.DMA(())],
        input_output_aliases={2: 0},
        compiler_params=_params("arbitrary"),
        name="moe_dispatch",
    )(pos3, xpk, xs0)


def _moe_body(te_ref, tr_ref, nu_ref, xs_ref, wg_ref, wl_ref, bg_ref, bl_ref, wd_ref, bd_ref, o_ref, acc_ref, *, sub):
    del te_ref
    i = pl.program_id(0)
    f = pl.program_id(1)
    tm, dh = xs_ref.shape

    @pl.when(i < nu_ref[0])
    def _():
        @pl.when(f == 0)
        def _():
            acc_ref[...] = jnp.broadcast_to(bd_ref[0, 0], acc_ref.shape)

        nsub = (tr_ref[i] + sub - 1) // sub

        def block(r0, m):
            lo, hi = _unpack_rows(xs_ref[pl.ds(r0, m), :])
            x = jnp.concatenate([lo.astype(BF16), hi.astype(BF16)], axis=1)
            hg = _dot(x, wg_ref[0, 0].astype(BF16)) + bg_ref[0, 0]
            hl = _dot(x, wl_ref[0, 0].astype(BF16)) + bl_ref[0, 0]
            glu = jnp.minimum(hg, SWIGLU_LIMIT)
            lin = jnp.clip(hl, -SWIGLU_LIMIT, SWIGLU_LIMIT)
            act = ((lin + 1.0) * glu * _sigmoid(SWIGLU_ALPHA * glu)).astype(BF16)
            acc_ref[pl.ds(r0, m), :] += _dot(act, wd_ref[0, 0].astype(BF16))

        def double_block(pi, carry):
            block(pl.multiple_of(pi * (2 * sub), 2 * sub), 2 * sub)
            return carry

        lax.fori_loop(0, nsub // 2, double_block, 0)

        @pl.when(nsub % 2 == 1)
        def _():
            block(pl.multiple_of((nsub - 1) * sub, sub), sub)

        @pl.when(f == pl.num_programs(1) - 1)
        def _():
            for s in range(tm // sub):
                o_ref[s * sub:(s + 1) * sub, :] = _pack_rows(acc_ref[s * sub:(s + 1) * sub, :])


def _moe_experts(xs, w_gu, b_gu, w_dn, b_dn, layer, tile_e, tile_rows, n_used, tm, tf, sub):
    n_rows, dh = xs.shape
    d = 2 * dh
    depth, e, _, ff2 = w_gu.shape
    ff = ff2 // 2
    tf = min(tf, ff)
    nf = ff // tf
    nt = n_rows // tm

    def tile_idx(i, f, te, tr, nu):
        return (jnp.minimum(i, nu[0] - 1), 0)

    def fidx(i, f, nu):
        return jnp.where(i < nu[0], f, nf - 1)

    grid_spec = pltpu.PrefetchScalarGridSpec(
        num_scalar_prefetch=3,
        grid=(nt, nf),
        in_specs=[
            pl.BlockSpec((tm, dh), tile_idx),
            pl.BlockSpec((1, 1, d, tf), lambda i, f, te, tr, nu: (layer, te[i], 0, fidx(i, f, nu))),
            pl.BlockSpec((1, 1, d, tf), lambda i, f, te, tr, nu: (layer, te[i], 0, nf + fidx(i, f, nu))),
            pl.BlockSpec((1, 1, 1, tf), lambda i, f, te, tr, nu: (layer, te[i], 0, fidx(i, f, nu))),
            pl.BlockSpec((1, 1, 1, tf), lambda i, f, te, tr, nu: (layer, te[i], 0, nf + fidx(i, f, nu))),
            pl.BlockSpec((1, 1, tf, d), lambda i, f, te, tr, nu: (layer, te[i], fidx(i, f, nu), 0)),
            pl.BlockSpec((1, 1, 1, d), lambda i, f, te, tr, nu: (layer, te[i], 0, 0)),
        ],
        out_specs=pl.BlockSpec((tm, dh), tile_idx),
        scratch_shapes=[pltpu.VMEM((tm, d), F32)],
    )
    return pl.pallas_call(
        functools.partial(_moe_body, sub=sub),
        grid_spec=grid_spec,
        out_shape=jax.ShapeDtypeStruct((n_rows, dh), I32),
        input_output_aliases={3: 0},
        compiler_params=_params("arbitrary", "arbitrary"),
        name="moe_experts",
    )(tile_e, tile_rows, n_used, xs, w_gu, w_gu, b_gu.reshape(depth, e, 1, ff2), b_gu.reshape(depth, e, 1, ff2), w_dn,
      b_dn.reshape(depth, e, 1, d))


def _combine_body(pos_ref, pos_next_ref, rg_ref, outs_ref, xres_ref, g_ref, b_ref, of_ref, ob_ref, buf_ref, sem, *,
                  alpha):
    i = pl.program_id(0)
    tc = rg_ref.shape[0]
    slot = i % 2

    def gather_rows(p_ref, dst_slot):
        def issue(t, carry):
            for k in range(TOP_K):
                pltpu.make_async_copy(outs_ref.at[pl.ds(p_ref[0, 0, TOP_K * t + k], 1)],
                                      buf_ref.at[dst_slot, k, pl.ds(t, 1)], sem.at[dst_slot]).start()
            return carry

        lax.fori_loop(0, tc, issue, 0, unroll=2)

    @pl.when(i == 0)
    def _():
        gather_rows(pos_ref, 0)

    @pl.when(i + 1 < pl.num_programs(0))
    def _():
        gather_rows(pos_next_ref, 1 - slot)

    for k in range(TOP_K):
        pltpu.make_async_copy(outs_ref.at[pl.ds(0, tc)], buf_ref.at[slot, k], sem.at[slot]).wait()
    gates = rg_ref[...]
    ylo = yhi = None
    for k in range(TOP_K):
        lo, hi = _unpack_rows(buf_ref[slot, k])
        gk = gates[:, k:k + 1]
        ylo = gk * lo if ylo is None else ylo + gk * lo
        yhi = gk * hi if yhi is None else yhi + gk * hi
    y = jnp.concatenate([ylo, yhi], axis=1)
    _ln_epilogue(y, xres_ref, g_ref, b_ref, alpha, (of_ref, ob_ref, None))


def _combine(outs, pos, rg, xres, g, b, alpha, tc=128):
    t, d = xres.shape
    tc = min(tc, t)
    dh = d // 2
    nsteps = t // tc
    pos3 = pos.reshape(nsteps, 1, tc * TOP_K)
    return pl.pallas_call(
        functools.partial(_combine_body, alpha=alpha),
        grid=(nsteps,),
        in_specs=[
            pl.BlockSpec((1, 1, tc * TOP_K), lambda i: (i, 0, 0), memory_space=pltpu.SMEM),
            pl.BlockSpec((1, 1, tc * TOP_K), lambda i: (jnp.minimum(i + 1, nsteps - 1), 0, 0), memory_space=pltpu.SMEM),
            pl.BlockSpec((tc, LANES), lambda i: (i, 0)),
            pl.BlockSpec(memory_space=pl.ANY),
            pl.BlockSpec((tc, d), lambda i: (i, 0)),
            pl.BlockSpec((1, d), lambda i: (0, 0)),
            pl.BlockSpec((1, d), lambda i: (0, 0)),
        ],
        out_specs=[pl.BlockSpec((tc, d), lambda i: (i, 0)), pl.BlockSpec((tc, d), lambda i: (i, 0))],
        out_shape=[jax.ShapeDtypeStruct((t, d), F32), jax.ShapeDtypeStruct((t, d), BF16)],
        scratch_shapes=[pltpu.VMEM((2, TOP_K, tc, dh), I32), pltpu.SemaphoreType.DMA((2,))],
        compiler_params=_params("arbitrary"),
        name="moe_combine_ln",
    )(pos3, pos3, rg, outs, xres, g, b)


def _moe_layer(xf, xpk, router_w, router_b, w_gu, b_gu, w_dn, b_dn, layer, g, b, alpha, tm, tf, sub):
    t, d = xf.shape
    e = router_w.shape[1]
    w_pad = jnp.zeros((d, LANES), F32).at[:, :e].set(router_w)
    b_pad = jnp.full((1, LANES), -1e30, F32).at[0, :e].set(router_b)
    ri, rg, cnt = _router(xf, w_pad, b_pad)
    counts = cnt[0, :e].astype(I32)
    nt_e = (counts + tm - 1) // tm
    per_e = ((counts + jnp.maximum(nt_e, 1) * sub - 1) // (jnp.maximum(nt_e, 1) * sub)) * sub
    per_e = jnp.maximum(per_e, sub)
    tile_end = jnp.cumsum(nt_e)
    tile_start = tile_end - nt_e
    n_used = tile_end[-1]
    nt = (t * TOP_K) // tm + e
    e_idx = ri[:, 0:TOP_K]
    rank = ri[:, TOP_K:2 * TOP_K]
    per_a = per_e[e_idx]
    tile_k = jnp.floor((rank.astype(F32) + 0.5) / per_a.astype(F32)).astype(I32)
    pos = (tile_start[e_idx] + tile_k) * tm + rank - tile_k * per_a
    tid = jnp.minimum(jnp.arange(nt, dtype=I32), n_used - 1)
    tile_e = jnp.minimum(jnp.searchsorted(tile_end, tid, side="right"), e - 1).astype(I32)
    tile_rows = jnp.clip(counts[tile_e] - (tid - tile_start[tile_e]) * per_e[tile_e], 0, per_e[tile_e]).astype(I32)
    xs = _dispatch(xpk, pos, nt * tm)
    outs = _moe_experts(xs, w_gu, b_gu, w_dn, b_dn, layer, tile_e, tile_rows, n_used.reshape(1).astype(I32), tm, tf,
                        sub)
    return _combine(outs, pos, rg, xf, g, b, alpha)


def _xattn_layer(xf, xb, mem_b, wq, wkv, wo, g, b, alpha, batch):
    q = _matmul(xb, wq.astype(BF16), BF16)
    kv = _matmul(mem_b, wkv.astype(BF16), BF16)
    return _xattn(q, kv, wo.astype(BF16), xf, g, b, alpha, batch)


def _even_mixer(xf, xb, w_in, conv_w, conv_b, a_log, dt_bias, d_skip, ssd_norm, gk_w2, gk_b, gla_norm, w_out, g, b,
                alpha, batch):
    d = xf.shape[1]
    grp = SSD_GROUPS
    heads = a_log.shape[1]
    r_heads = heads // grp
    d_ssd = heads * SSD_HEAD_DIM
    n_state = (conv_w.shape[1] - d_ssd) // (2 * grp)
    conv_ch = d_ssd + 2 * grp * n_state
    rank = gk_w2.shape[1]
    dk = gk_w2.shape[2] // GLA_HEADS
    dv = gla_norm.shape[0]
    o_xbc = d_ssd
    o_dt = o_xbc + conv_ch
    o_q = o_dt + 2 * heads
    o_k = o_q + GLA_HEADS * dk
    o_v = o_k + GLA_HEADS * dk
    o_g = o_v + GLA_HEADS * dv
    o_lr = o_g + GLA_HEADS * dv
    assert 2 * r_heads <= LANES and 2 * rank <= LANES and o_lr + 2 * rank == w_in.shape[1]
    w_big = jnp.concatenate([w_in[:, :o_dt], w_in[:, o_q:o_lr]], axis=1).astype(BF16)
    w_dt = w_in[:, o_dt:o_q].reshape(d, 2, grp, r_heads).transpose(0, 2, 1, 3).reshape(d, grp, 2 * r_heads)
    w_dt = jnp.pad(w_dt, ((0, 0), (0, 0), (0, LANES - 2 * r_heads))).reshape(d, grp * LANES)
    w_lr = jnp.pad(w_in[:, o_lr:], ((0, 0), (0, LANES - 2 * rank)))
    w_small = jnp.concatenate([w_dt, w_lr], axis=1).astype(BF16)
    proj = _matmul(xb, w_big, BF16)
    psmall = _matmul(xb, w_small, F32)

    xbc = _dwconv(proj, o_xbc, conv_ch, conv_w, conv_b, batch, act=True)

    def group_lanes(p):
        p = p.reshape(2, grp, r_heads).transpose(1, 0, 2).reshape(grp, 1, 2 * r_heads)
        return jnp.pad(p, ((0, 0), (0, 0), (0, LANES - 2 * r_heads)))

    y = _ssd(xbc, proj, psmall, group_lanes(a_log), group_lanes(dt_bias),
             jnp.repeat(d_skip, SSD_HEAD_DIM).reshape(1, d_ssd), ssd_norm.reshape(1, d_ssd), batch, d_ssd, n_state)

    w2 = gk_w2.reshape(2, rank, GLA_HEADS, dk).transpose(2, 0, 1, 3)
    w2pad = jnp.zeros((GLA_HEADS, 2, LANES, dk), F32)
    for dirn in range(2):
        w2pad = w2pad.at[:, dirn, dirn * rank:(dirn + 1) * rank, :].set(w2[:, dirn])
    w2pad = w2pad.reshape(GLA_HEADS, 2 * LANES, dk)
    gkb = gk_b.reshape(2, GLA_HEADS, dk).transpose(1, 0, 2)
    q0 = o_dt
    k0 = q0 + GLA_HEADS * dk
    v0 = k0 + GLA_HEADS * dk
    g0 = v0 + GLA_HEADS * dv
    o = _gla(proj, psmall, w2pad, gkb, gla_norm.reshape(1, dv), batch, q0, k0, v0, g0, grp, dk, dv)
    return _even_out(y, o, w_out.astype(BF16), xf, g, b, alpha)


def _odd_mixer(xf, xb, w_in, conv_w, conv_b, wa, ba, wx, bx, lam, w_out, g, b, alpha, batch):
    width = conv_w.shape[1]
    proj = _matmul(xb, w_in.astype(BF16), BF16)
    vc = _dwconv(proj, width, width, conv_w, conv_b, batch, act=False)
    hf, hb = _rglru(vc, wa.astype(BF16), wx.astype(BF16), ba, bx, lam, batch)
    return _odd_out(proj, hf, hb, w_out.astype(BF16), xf, g, b, alpha)


def kernel(x, mem, ln_g, ln_b, even_w_in, even_conv_w, even_conv_b, ssd_a_log, ssd_dt_bias, ssd_d, ssd_norm, gla_gk_w2, gla_gk_b, gla_norm, even_w_out, odd_w_in, odd_conv_w, odd_conv_b, rg_wa, rg_ba, rg_wx, rg_bx, rg_lam, odd_w_out, xattn_wq, xattn_wkv, xattn_wo, router_w, router_b, moe_w_gu, moe_b_gu, moe_w_dn, moe_b_dn):
    batch, seq, d = x.shape
    depth = ln_g.shape[0]
    alpha = float((2 * depth) ** 0.25)
    xf = x.reshape(batch * seq, d)
    xb = xf.astype(BF16)
    mem_b = mem.reshape(-1, d).astype(BF16)
    for layer in range(depth):
        i = layer // 2
        lg = lambda s: ln_g[layer, s].reshape(1, d)
        lb = lambda s: ln_b[layer, s].reshape(1, d)
        if layer % 2 == 0:
            xf, xb = _even_mixer(xf, xb, even_w_in[i], even_conv_w[i], even_conv_b[i], ssd_a_log[i], ssd_dt_bias[i],
                                 ssd_d[i], ssd_norm[i], gla_gk_w2[i], gla_gk_b[i], gla_norm[i], even_w_out[i],
                                 lg(0), lb(0), alpha, batch)
        else:
            xf, xb = _odd_mixer(xf, xb, odd_w_in[i], odd_conv_w[i], odd_conv_b[i], rg_wa[i], rg_ba[i], rg_wx[i],
                                rg_bx[i], rg_lam[i], odd_w_out[i], lg(0), lb(0), alpha, batch)
        xf, xpk = _xattn_layer(xf, xb, mem_b, xattn_wq[layer], xattn_wkv[layer], xattn_wo[layer], lg(1), lb(1), alpha,
                               batch)
        xf, xb = _moe_layer(xf, xpk, router_w[layer], router_b[layer], moe_w_gu, moe_b_gu, moe_w_dn, moe_b_dn, layer,
                            lg(2), lb(2), alpha, MOE_TILE, MOE_FF_TILE, MOE_SUB)
    return xf.reshape(batch, seq, d)
```
